```python
import math
import jax, jax.numpy as jnp
from jax import lax
import numpy as np

D_MODEL = 2048
BATCH = 4
SEQ = 2048
DEPTH = 1

SSM_WIDTH = D_MODEL // 2
SSM_GROUP = 16
SSM_GROUPS = SSM_WIDTH // SSM_GROUP
SSM_STATE = 64
DT_MIN = 0.001
DT_MAX = 0.1
POOL_WIDTH = D_MODEL // 2
POOL_WINDOWS = (2, 4, 8, 16)
POOL_GROUP = POOL_WIDTH // len(POOL_WINDOWS)
N_BRANCH = 2
IN_WIDTH = SSM_WIDTH + POOL_WIDTH + N_BRANCH * D_MODEL
D_FF = 4 * D_MODEL
N_MOD = 6
LN_EPS = 1e-5
ALPHA = (2.0 * DEPTH) ** 0.25
BETA = (8.0 * DEPTH) ** -0.25

kernel_name = 'hybrid_s5_pool_gated_postnorm_block'


def _layernorm(x, gain=None, bias=None):
    xf = x.astype(jnp.float32)
    mu = jnp.mean(xf, axis=-1, keepdims=True)
    var = jnp.mean(jnp.square(xf - mu), axis=-1, keepdims=True)
    y = (xf - mu) * lax.rsqrt(var + LN_EPS)
    if gain is not None:
        y = y * gain.astype(jnp.float32) + bias.astype(jnp.float32)
    return y.astype(x.dtype)


def _modulate(x, shift, scale):
    return _layernorm(x) * (1.0 + scale) + shift


def _s5_branch(u, lam_re, lam_im, log_dt, b_re, b_im, c_re, c_im, d_skip, w_val, w_gate):
    f32 = jnp.float32
    bsz, seq, _ = u.shape
    uf = u.astype(f32).reshape(bsz, seq, SSM_GROUPS, SSM_GROUP)
    lr = lam_re.astype(f32)
    li = lam_im.astype(f32)
    dt = jnp.exp(log_dt.astype(f32))[:, None]
    mag = jnp.exp(lr * dt)
    ang = li * dt
    ab_re = mag * jnp.cos(ang)
    ab_im = mag * jnp.sin(ang)
    num_re = ab_re - 1.0
    num_im = ab_im
    den = lr * lr + li * li
    f_re = (num_re * lr + num_im * li) / den
    f_im = (num_im * lr - num_re * li) / den
    br = b_re.astype(f32)
    bi = b_im.astype(f32)
    bb_re = f_re[..., None] * br - f_im[..., None] * bi
    bb_im = f_re[..., None] * bi + f_im[..., None] * br
    bu_re = jnp.einsum('bsgh,gph->bsgp', uf, bb_re)
    bu_im = jnp.einsum('bsgh,gph->bsgp', uf, bb_im)
    a_re = jnp.broadcast_to(ab_re, bu_re.shape)
    a_im = jnp.broadcast_to(ab_im, bu_im.shape)

    def combine(e1, e2):
        a1r, a1i, b1r, b1i = e1
        a2r, a2i, b2r, b2i = e2
        return (a2r * a1r - a2i * a1i,
                a2r * a1i + a2i * a1r,
                a2r * b1r - a2i * b1i + b2r,
                a2r * b1i + a2i * b1r + b2i)

    _, _, xs_re, xs_im = lax.associative_scan(combine, (a_re, a_im, bu_re, bu_im), axis=1)
    y = (jnp.einsum('bsgp,ghp->bsgh', xs_re, c_re.astype(f32))
         - jnp.einsum('bsgp,ghp->bsgh', xs_im, c_im.astype(f32))
         + d_skip.astype(f32).reshape(SSM_GROUPS, SSM_GROUP) * uf)
    y = y.reshape(bsz, seq, SSM_WIDTH).astype(u.dtype)
    z = jax.nn.gelu(y)
    return (z @ w_val) * jax.nn.sigmoid(z @ w_gate)


def _pool_branch(u, w_pool, pool_scale, w_pool_out):
    f32 = jnp.float32
    seq = u.shape[1]
    uf = u.astype(f32)
    cs = jnp.pad(jnp.cumsum(uf, axis=1), ((0, 0), (1, 0), (0, 0)))
    pos = jnp.arange(seq, dtype=f32)
    outs = []
    for gi, w in enumerate(POOL_WINDOWS):
        lo_c, hi_c = gi * POOL_GROUP, (gi + 1) * POOL_GROUP
        csg = cs[..., lo_c:hi_c]
        hi = csg[:, 1:]
        lo = jnp.pad(csg, ((0, 0), (w - 1, 0), (0, 0)))[:, :seq]
        count = jnp.minimum(pos + 1.0, float(w))[None, :, None]
        pooled = (hi - lo) / count - uf[..., lo_c:hi_c]
        outs.append(jnp.einsum('bsc,cd->bsd', pooled, w_pool[gi].astype(f32)))
    y = jnp.concatenate(outs, axis=-1) * pool_scale.astype(f32)
    return y.astype(u.dtype) @ w_pool_out


def _hybrid_mixer(h, w_in, lam_re, lam_im, log_dt, b_re, b_im, c_re, c_im, d_skip,
                  w_glu_val, w_glu_gate, w_pool, pool_scale, w_pool_out, w_out):
    proj = h @ w_in
    u_ssm = proj[..., :SSM_WIDTH]
    u_pool = proj[..., SSM_WIDTH:SSM_WIDTH + POOL_WIDTH]
    g_a = proj[..., SSM_WIDTH + POOL_WIDTH:SSM_WIDTH + POOL_WIDTH + D_MODEL]
    g_b = proj[..., SSM_WIDTH + POOL_WIDTH + D_MODEL:]
    y_a = _s5_branch(u_ssm, lam_re, lam_im, log_dt, b_re, b_im, c_re, c_im, d_skip,
                     w_glu_val, w_glu_gate)
    y_b = _pool_branch(u_pool, w_pool, pool_scale, w_pool_out)
    merged = jax.nn.sigmoid(g_a) * y_a + jax.nn.sigmoid(g_b) * y_b
    return merged @ w_out


def _sq_relu_mlp(h, w_ff1, w_ff2):
    return jnp.square(jax.nn.relu(h @ w_ff1)) @ w_ff2


def setup_inputs(seed: int = 0) -> dict:
    key = jax.random.key(seed)
    ks = jax.random.split(key, 26)
    f32 = jnp.float32
    L = DEPTH

    def nrm(k, shape, scale):
        return jax.random.normal(k, shape, f32) * scale

    n_idx = jnp.arange(SSM_STATE, dtype=f32)
    return {
        'x': nrm(ks[0], (BATCH, SEQ, D_MODEL), 1.0),
        'c': nrm(ks[1], (BATCH, D_MODEL), 1.0),
        'w_ada': nrm(ks[2], (L, D_MODEL, N_MOD * D_MODEL), 0.5 * D_MODEL ** -0.5),
        'b_ada': nrm(ks[3], (L, N_MOD * D_MODEL), 0.02),
        'w_in': nrm(ks[4], (L, D_MODEL, IN_WIDTH), D_MODEL ** -0.5),
        'lam_re': -0.5 + nrm(ks[5], (L, SSM_GROUPS, SSM_STATE), 0.01),
        'lam_im': math.pi * n_idx + nrm(ks[6], (L, SSM_GROUPS, SSM_STATE), 0.01),
        'log_dt': jax.random.uniform(ks[7], (L, SSM_GROUPS), f32,
                                     math.log(DT_MIN), math.log(DT_MAX)),
        'ssm_b_re': nrm(ks[8], (L, SSM_GROUPS, SSM_STATE, SSM_GROUP), (2.0 * SSM_GROUP) ** -0.5),
        'ssm_b_im': nrm(ks[9], (L, SSM_GROUPS, SSM_STATE, SSM_GROUP), (2.0 * SSM_GROUP) ** -0.5),
        'ssm_c_re': nrm(ks[10], (L, SSM_GROUPS, SSM_GROUP, SSM_STATE), (2.0 * SSM_STATE) ** -0.5),
        'ssm_c_im': nrm(ks[11], (L, SSM_GROUPS, SSM_GROUP, SSM_STATE), (2.0 * SSM_STATE) ** -0.5),
        'ssm_d': nrm(ks[12], (L, SSM_WIDTH), 1.0),
        'w_glu_val': nrm(ks[13], (L, SSM_WIDTH, D_MODEL), BETA * SSM_WIDTH ** -0.5),
        'w_glu_gate': nrm(ks[14], (L, SSM_WIDTH, D_MODEL), SSM_WIDTH ** -0.5),
        'w_pool': nrm(ks[15], (L, len(POOL_WINDOWS), POOL_GROUP, POOL_GROUP), POOL_GROUP ** -0.5),
        'pool_scale': 1.0 + nrm(ks[16], (L, POOL_WIDTH), 0.1),
        'w_pool_out': nrm(ks[17], (L, POOL_WIDTH, D_MODEL), BETA * POOL_WIDTH ** -0.5),
        'w_out': nrm(ks[18], (L, D_MODEL, D_MODEL), BETA * D_MODEL ** -0.5),
        'ln1_g': 1.0 + nrm(ks[19], (L, D_MODEL), 0.02),
        'ln1_b': nrm(ks[20], (L, D_MODEL), 0.02),
        'w_ff1': nrm(ks[21], (L, D_MODEL, D_FF), D_MODEL ** -0.5),
        'w_ff2': nrm(ks[22], (L, D_FF, D_MODEL), BETA * D_FF ** -0.5),
        'ln2_g': 1.0 + nrm(ks[23], (L, D_MODEL), 0.02),
        'ln2_b': nrm(ks[24], (L, D_MODEL), 0.02),
    }


def reference(x, c, w_ada, b_ada, w_in, lam_re, lam_im, log_dt, ssm_b_re, ssm_b_im,
              ssm_c_re, ssm_c_im, ssm_d, w_glu_val, w_glu_gate, w_pool, pool_scale,
              w_pool_out, w_out, ln1_g, ln1_b, w_ff1, w_ff2, ln2_g, ln2_b):
    c_act = jax.nn.silu(c)
    for l in range(DEPTH):
        mod = (c_act @ w_ada[l] + b_ada[l])[:, None, :]
        sh1, sc1, g1, sh2, sc2, g2 = jnp.split(mod, N_MOD, axis=-1)
        h = _modulate(x, sh1, sc1)
        y = _hybrid_mixer(h, w_in[l], lam_re[l], lam_im[l], log_dt[l], ssm_b_re[l], ssm_b_im[l],
                          ssm_c_re[l], ssm_c_im[l], ssm_d[l], w_glu_val[l], w_glu_gate[l],
                          w_pool[l], pool_scale[l], w_pool_out[l], w_out[l])
        x = _layernorm(ALPHA * x + g1 * y, ln1_g[l], ln1_b[l])
        h = _modulate(x, sh2, sc2)
        y = _sq_relu_mlp(h, w_ff1[l], w_ff2[l])
        x = _layernorm(ALPHA * x + g2 * y, ln2_g[l], ln2_b[l])
    return x
```

```python
import functools
import math

import jax
import jax.numpy as jnp
from jax import lax
from jax.experimental import pallas as pl
from jax.experimental.pallas import tpu as pltpu

D_MODEL = 2048
SSM_WIDTH = 1024
SSM_GROUP = 16
SSM_GROUPS = 64
SSM_STATE = 64
POOL_WIDTH = 1024
POOL_WINDOWS = (2, 4, 8, 16)
POOL_GROUP = 256
IN_WIDTH = SSM_WIDTH + POOL_WIDTH + 2 * D_MODEL
D_FF = 4 * D_MODEL
N_MOD = 6
LN_EPS = 1e-5
ALPHA = 2.0 ** 0.25

SUBLANES = 8
LANES = 128
VMEM_LIMIT = 56 * 1024 * 1024

SSM_CHUNK = 128
SSM_ROWS = SUBLANES * SSM_CHUNK
SSM_GB = 4
SSM_CH = SSM_WIDTH // SSM_GB
SSM_ST = SSM_CH // SSM_GROUP * SSM_STATE
SSM_COLS = 512

BF16 = jnp.bfloat16
F32 = jnp.float32


def _cparams(sem):
    return pltpu.CompilerParams(dimension_semantics=sem, vmem_limit_bytes=VMEM_LIMIT)


def _ln(x):
    mu = jnp.mean(x, axis=-1, keepdims=True)
    xc = x - mu
    var = jnp.mean(xc * xc, axis=-1, keepdims=True)
    return xc * lax.rsqrt(var + LN_EPS)


def _mod_kernel(c_ref, w_ref, b_ref, o_ref):
    c = c_ref[...]
    c_act = (c * jax.nn.sigmoid(c)).astype(BF16)
    o_ref[...] = jnp.dot(c_act, w_ref[...].astype(BF16),
                         preferred_element_type=F32) + b_ref[...]


def _mod(c_pad, w_ada, b_ada):
    tn = 1024
    n = w_ada.shape[1]
    return pl.pallas_call(
        _mod_kernel,
        grid=(n // tn,),
        in_specs=[pl.BlockSpec((SUBLANES, D_MODEL), lambda j: (0, 0)),
                  pl.BlockSpec((D_MODEL, tn), lambda j: (0, j)),
                  pl.BlockSpec((1, tn), lambda j: (0, j))],
        out_specs=pl.BlockSpec((SUBLANES, tn), lambda j: (0, j)),
        out_shape=jax.ShapeDtypeStruct((SUBLANES, n), F32),
        compiler_params=_cparams(("arbitrary",)),
        name="mod",
    )(c_pad, w_ada, b_ada)


def _in_proj_kernel(x_ref, sh_ref, sc_ref, w_ref, proj_ref, ussm_ref, h_ref):
    j = pl.program_id(1)

    @pl.when(j == 0)
    def _():
        y = _ln(x_ref[...])
        h_ref[...] = (y * (1.0 + sc_ref[0]) + sh_ref[0]).astype(BF16)

    acc = jnp.dot(h_ref[...], w_ref[...], preferred_element_type=F32)
    proj_ref[...] = acc.astype(BF16)

    @pl.when(j == 0)
    def _():
        ussm_ref[...] = acc


def _in_proj(x2, sh, sc, w_in, seq):
    tm, tn = 1024, 1024
    t = x2.shape[0]
    per_batch = seq // tm
    return pl.pallas_call(
        _in_proj_kernel,
        grid=(t // tm, IN_WIDTH // tn),
        in_specs=[pl.BlockSpec((tm, D_MODEL), lambda i, j: (i, 0)),
                  pl.BlockSpec((1, 1, D_MODEL), lambda i, j: (i // per_batch, 0, 0)),
                  pl.BlockSpec((1, 1, D_MODEL), lambda i, j: (i // per_batch, 0, 0)),
                  pl.BlockSpec((D_MODEL, tn), lambda i, j: (0, j))],
        out_specs=[pl.BlockSpec((tm, tn), lambda i, j: (i, j)),
                   pl.BlockSpec((tm, SSM_WIDTH), lambda i, j: (i, 0))],
        out_shape=[jax.ShapeDtypeStruct((t, IN_WIDTH), BF16),
                   jax.ShapeDtypeStruct((t, SSM_WIDTH), F32)],
        scratch_shapes=[pltpu.VMEM((tm, D_MODEL), BF16)],
        compiler_params=_cparams(("parallel", "arbitrary")),
        name="in_proj",
    )(x2, sh, sc, w_in)


def _ssm_prep_kernel(lr_ref, li_ref, ldt_ref, br_ref, bi_ref,
                     are_ref, aim_ref, alre_ref, alim_ref, bbre_ref, bbim_ref):
    lr = lr_ref[...]
    li = li_ref[...]
    dt = jnp.exp(ldt_ref[...])
    mag = jnp.exp(lr * dt)
    ang = li * dt
    ab_re = mag * jnp.cos(ang)
    ab_im = mag * jnp.sin(ang)
    are_ref[...] = ab_re
    aim_ref[...] = ab_im
    mag_l = jnp.exp(lr * dt * float(SSM_CHUNK))
    ang_l = ang * float(SSM_CHUNK)
    alre_ref[...] = mag_l * jnp.cos(ang_l)
    alim_ref[...] = mag_l * jnp.sin(ang_l)
    num_re = ab_re - 1.0
    num_im = ab_im
    den = lr * lr + li * li
    f_re = (num_re * lr + num_im * li) / den
    f_im = (num_im * lr - num_re * li) / den
    br = br_ref[...]
    bi = bi_ref[...]
    bbre_ref[...] = f_re * br - f_im * bi
    bbim_ref[...] = f_re * bi + f_im * br


def _ssm_prep(lam_re, lam_im, log_dt, b_re, b_im):
    rep = lambda a: jnp.repeat(a, SSM_GROUP, axis=0)
    lr = rep(lam_re)
    li = rep(lam_im)
    ldt = rep(jnp.broadcast_to(log_dt[:, None], (SSM_GROUPS, SSM_STATE)))
    br = b_re.transpose(0, 2, 1).reshape(SSM_WIDTH, SSM_STATE)
    bi = b_im.transpose(0, 2, 1).reshape(SSM_WIDTH, SSM_STATE)
    shp = jax.ShapeDtypeStruct((SSM_WIDTH, SSM_STATE), F32)
    return pl.pallas_call(
        _ssm_prep_kernel,
        out_shape=[shp] * 6,
        name="ssm_prep",
    )(lr, li, ldt, br, bi)


def _ssm_kernel(u0_ref, u1_ref, wb_ref, wc_ref, are_ref, aim_ref, alre_ref, alim_ref,
                d_ref, z_ref, up_ref, upb_ref, bu_ref, xb_ref, cst_ref, st_ref,
                ys0_ref, ys1_ref):
    k = pl.program_id(2)
    L = SSM_CHUNK

    @pl.when(k == 0)
    def _():
        cst_ref[...] = jnp.zeros_like(cst_ref)

    def permute_in(l, c):
        rows = pl.ds(pl.multiple_of(l * SUBLANES, SUBLANES), SUBLANES)
        up_ref[rows, 0:LANES] = u0_ref[pl.ds(l, SUBLANES, stride=L), :]
        up_ref[rows, LANES:2 * LANES] = u1_ref[pl.ds(l, SUBLANES, stride=L), :]
        return c
    lax.fori_loop(0, L, permute_in, 0)

    upb_ref[...] = up_ref[...].astype(BF16)
    for n in range(2 * SSM_ST // 256):
        bu_ref[:, n * 256:(n + 1) * 256] = jnp.dot(
            upb_ref[...], wb_ref[0, :, n * 256:(n + 1) * 256],
            preferred_element_type=F32)

    row_id = lax.broadcasted_iota(jnp.int32, (SUBLANES, SSM_COLS), 0)
    for cg in range(SSM_ST // SSM_COLS):
        re_sl = slice(cg * SSM_COLS, (cg + 1) * SSM_COLS)
        im_sl = slice(SSM_ST + cg * SSM_COLS, SSM_ST + (cg + 1) * SSM_COLS)
        ar = are_ref[0, :, re_sl]
        ai = aim_ref[0, :, re_sl]

        def step(l, sr, si):
            row = pl.multiple_of(l * SUBLANES, SUBLANES)
            br = bu_ref[pl.ds(row, SUBLANES), re_sl]
            bi = bu_ref[pl.ds(row, SUBLANES), im_sl]
            return ar * sr - ai * si + br, ar * si + ai * sr + bi

        def pass1(l, carry):
            return step(l, *carry)
        er, ei = lax.fori_loop(0, L, pass1, (cst_ref[:, re_sl], cst_ref[:, im_sl]),
                               unroll=2)

        alr = alre_ref[0, :, re_sl]
        ali = alim_ref[0, :, re_sl]
        tr, ti = er, ei
        for r in range(1, SUBLANES):
            pr = pltpu.roll(tr, 1, 0)
            pi = pltpu.roll(ti, 1, 0)
            tr = jnp.where(row_id == r, er + alr * pr - ali * pi, tr)
            ti = jnp.where(row_id == r, ei + alr * pi + ali * pr, ti)
        pr = pltpu.roll(tr, 1, 0)
        pi = pltpu.roll(ti, 1, 0)
        first = row_id == 0
        st_ref[:, re_sl] = jnp.where(first, cst_ref[:, re_sl], pr)
        st_ref[:, im_sl] = jnp.where(first, cst_ref[:, im_sl], pi)
        cst_ref[:, re_sl] = jnp.where(first, pr, 0.0)
        cst_ref[:, im_sl] = jnp.where(first, pi, 0.0)

        def pass2(m, carry):
            sr, si = carry
            r1, i1 = step(2 * m, sr, si)
            r2, i2 = step(2 * m + 1, r1, i1)
            row = pl.multiple_of(m * 2 * SUBLANES, 2 * SUBLANES)
            xb_ref[pl.ds(row, 2 * SUBLANES), re_sl] = (
                jnp.concatenate([r1, r2], axis=0).astype(BF16))
            xb_ref[pl.ds(row, 2 * SUBLANES), im_sl] = (
                jnp.concatenate([i1, i2], axis=0).astype(BF16))
            return r2, i2
        lax.fori_loop(0, L // 2, pass2, (st_ref[:, re_sl], st_ref[:, im_sl]))

    mrows = 256
    for m in range(SSM_ROWS // mrows):
        rs = slice(m * mrows, (m + 1) * mrows)
        y = jnp.dot(xb_ref[rs, :], wc_ref[0], preferred_element_type=F32)
        y = y + d_ref[0] * up_ref[rs, :]
        ys0_ref[rs, :] = y[:, 0:LANES]
        ys1_ref[rs, :] = y[:, LANES:2 * LANES]

    for r in range(SUBLANES):
        for h, ys_ref in enumerate((ys0_ref, ys1_ref)):
            y = ys_ref[pl.ds(r, L, stride=SUBLANES), :]
            z_ref[r * L:(r + 1) * L, h * LANES:(h + 1) * LANES] = (
                jax.nn.gelu(y, approximate=True).astype(BF16))


def _ssm(u, wb, wc, a_re, a_im, al_re, al_im, d, seq):
    t = u.shape[0]
    nb = t // seq
    nk = seq // SSM_ROWS
    tok = lambda g, b, k: (b * nk + k, g)
    par = lambda g, b, k: (g, 0, 0)
    return pl.pallas_call(
        _ssm_kernel,
        grid=(SSM_GB, nb, nk),
        in_specs=[pl.BlockSpec((SSM_ROWS, LANES), lambda g, b, k: (b * nk + k, 2 * g)),
                  pl.BlockSpec((SSM_ROWS, LANES), lambda g, b, k: (b * nk + k, 2 * g + 1)),
                  pl.BlockSpec((1, SSM_CH, 2 * SSM_ST), par),
                  pl.BlockSpec((1, 2 * SSM_ST, SSM_CH), par),
                  pl.BlockSpec((1, SUBLANES, SSM_ST), par),
                  pl.BlockSpec((1, SUBLANES, SSM_ST), par),
                  pl.BlockSpec((1, SUBLANES, SSM_ST), par),
                  pl.BlockSpec((1, SUBLANES, SSM_ST), par),
                  pl.BlockSpec((1, 1, SSM_CH), par)],
        out_specs=pl.BlockSpec((SSM_ROWS, SSM_CH), tok),
        out_shape=jax.ShapeDtypeStruct((t, SSM_WIDTH), BF16),
        scratch_shapes=[pltpu.VMEM((SSM_ROWS, SSM_CH), F32),
                        pltpu.VMEM((SSM_ROWS, SSM_CH), BF16),
                        pltpu.VMEM((SSM_ROWS, 2 * SSM_ST), F32),
                        pltpu.VMEM((SSM_ROWS, 2 * SSM_ST), BF16),
                        pltpu.VMEM((SUBLANES, 2 * SSM_ST), F32),
                        pltpu.VMEM((SUBLANES, 2 * SSM_ST), F32),
                        pltpu.VMEM((SSM_ROWS, LANES), F32),
                        pltpu.VMEM((SSM_ROWS, LANES), F32)],
        compiler_params=_cparams(("parallel", "parallel", "arbitrary")),
        name="ssm",
    )(u, u, wb, wc, a_re, a_im, al_re, al_im, d)


POOL_HALO = 16


def _pool_kernel(u_ref, w_ref, s_ref, o_ref, ext_ref, *, tt):
    k = pl.program_id(1)

    @pl.when(k == 0)
    def _():
        ext_ref[0:POOL_HALO, :] = jnp.zeros((POOL_HALO, POOL_WIDTH), F32)

    @pl.when(k != 0)
    def _():
        ext_ref[0:POOL_HALO, :] = ext_ref[tt:tt + POOL_HALO, :]

    ext_ref[POOL_HALO:POOL_HALO + tt, :] = u_ref[...].astype(F32)
    pos = (k * tt + lax.broadcasted_iota(jnp.int32, (tt, POOL_GROUP), 0)).astype(F32)
    for gi, w in enumerate(POOL_WINDOWS):
        cs = slice(gi * POOL_GROUP, (gi + 1) * POOL_GROUP)
        cur = ext_ref[POOL_HALO:POOL_HALO + tt, cs]
        tot = cur
        for j in range(1, w):
            tot = tot + ext_ref[POOL_HALO - j:POOL_HALO - j + tt, cs]
        count = jnp.minimum(pos + 1.0, float(w))
        pooled = tot / count - cur
        y = jnp.dot(pooled.astype(BF16), w_ref[gi], preferred_element_type=F32)
        o_ref[:, cs] = (y * s_ref[:, cs]).astype(BF16)


def _pool(proj, w_pool, pool_scale, seq):
    tt = 512
    t = proj.shape[0]
    nk = seq // tt
    return pl.pallas_call(
        functools.partial(_pool_kernel, tt=tt),
        grid=(t // seq, nk),
        in_specs=[pl.BlockSpec((tt, POOL_WIDTH), lambda b, k: (b * nk + k, 1)),
                  pl.BlockSpec((len(POOL_WINDOWS), POOL_GROUP, POOL_GROUP),
                               lambda b, k: (0, 0, 0)),
                  pl.BlockSpec((1, POOL_WIDTH), lambda b, k: (0, 0))],
        out_specs=pl.BlockSpec((tt, POOL_WIDTH), lambda b, k: (b * nk + k, 0)),
        out_shape=jax.ShapeDtypeStruct((t, POOL_WIDTH), BF16),
        scratch_shapes=[pltpu.VMEM((tt + POOL_HALO, POOL_WIDTH), F32)],
        compiler_params=_cparams(("parallel", "arbitrary")),
        name="pool",
    )(proj, w_pool, pool_scale)


def _merge_kernel(z_ref, yp_ref, ga_ref, gb_ref, wv_ref, wg_ref, wp_ref, o_ref):
    z = z_ref[...]
    val = jnp.dot(z, wv_ref[...], preferred_element_type=F32)
    gate = jnp.dot(z, wg_ref[...], preferred_element_type=F32)
    y_a = val * jax.nn.sigmoid(gate)
    y_b = jnp.dot(yp_ref[...], wp_ref[...], preferred_element_type=F32)
    merged = (jax.nn.sigmoid(ga_ref[...].astype(F32)) * y_a
              + jax.nn.sigmoid(gb_ref[...].astype(F32)) * y_b)
    o_ref[...] = merged.astype(BF16)


def _merge(z, yp, proj, w_val, w_gate, w_po):
    tm, tn = 1024, 512
    t = z.shape[0]
    ga0 = (SSM_WIDTH + POOL_WIDTH) // tn
    gb0 = ga0 + D_MODEL // tn
    wspec = pl.BlockSpec((SSM_WIDTH, tn), lambda i, j: (0, j))
    return pl.pallas_call(
        _merge_kernel,
        grid=(t // tm, D_MODEL // tn),
        in_specs=[pl.BlockSpec((tm, SSM_WIDTH), lambda i, j: (i, 0)),
                  pl.BlockSpec((tm, POOL_WIDTH), lambda i, j: (i, 0)),
                  pl.BlockSpec((tm, tn), lambda i, j: (i, ga0 + j)),
                  pl.BlockSpec((tm, tn), lambda i, j: (i, gb0 + j)),
                  wspec, wspec, wspec],
        out_specs=pl.BlockSpec((tm, tn), lambda i, j: (i, j)),
        out_shape=jax.ShapeDtypeStruct((t, D_MODEL), BF16),
        compiler_params=_cparams(("parallel", "arbitrary")),
        name="merge",
    )(z, yp, proj, proj, w_val, w_gate, w_po)


def _out_ln_kernel(m_ref, w_ref, x_ref, g1_ref, lg_ref, lb_ref, sh_ref, sc_ref,
                   x1_ref, h2_ref):
    y = jnp.dot(m_ref[...], w_ref[...], preferred_element_type=F32)
    r = ALPHA * x_ref[...] + g1_ref[0] * y
    x1 = _ln(r) * lg_ref[...] + lb_ref[...]
    x1_ref[...] = x1
    h2_ref[...] = (_ln(x1) * (1.0 + sc_ref[0]) + sh_ref[0]).astype(BF16)


def _out_ln(merged, w_out, x2, g1, ln_g, ln_b, sh2, sc2, seq):
    tm = 512
    t = x2.shape[0]
    per_batch = seq // tm
    row = pl.BlockSpec((tm, D_MODEL), lambda i: (i, 0))
    vec = pl.BlockSpec((1, D_MODEL), lambda i: (0, 0))
    bvec = pl.BlockSpec((1, 1, D_MODEL), lambda i: (i // per_batch, 0, 0))
    return pl.pallas_call(
        _out_ln_kernel,
        grid=(t // tm,),
        in_specs=[row, pl.BlockSpec((D_MODEL, D_MODEL), lambda i: (0, 0)), row,
                  bvec, vec, vec, bvec, bvec],
        out_specs=[row, row],
        out_shape=[jax.ShapeDtypeStruct((t, D_MODEL), F32),
                   jax.ShapeDtypeStruct((t, D_MODEL), BF16)],
        compiler_params=_cparams(("parallel",)),
        name="out_ln",
    )(merged, w_out, x2, g1, ln_g, ln_b, sh2, sc2)


def _mlp_kernel(h_ref, w1_ref, w2_ref, x1_ref, g2_ref, lg_ref, lb_ref, o_ref, acc_ref):
    f = pl.program_id(1)
    a = jnp.maximum(jnp.dot(h_ref[...], w1_ref[...], preferred_element_type=F32), 0.0)
    part = jnp.dot((a * a).astype(BF16), w2_ref[...], preferred_element_type=F32)

    @pl.when(f == 0)
    def _():
        acc_ref[...] = part

    @pl.when(f != 0)
    def _():
        acc_ref[...] += part

    @pl.when(f == pl.num_programs(1) - 1)
    def _():
        r = ALPHA * x1_ref[...] + g2_ref[0] * acc_ref[...]
        o_ref[...] = _ln(r) * lg_ref[...] + lb_ref[...]


def _mlp(h2, w1, w2, x1, g2, ln_g, ln_b, seq):
    tm, tf = 512, 1024
    t = h2.shape[0]
    per_batch = seq // tm
    row = pl.BlockSpec((tm, D_MODEL), lambda i, f: (i, 0))
    vec = pl.BlockSpec((1, D_MODEL), lambda i, f: (0, 0))
    return pl.pallas_call(
        _mlp_kernel,
        grid=(t // tm, D_FF // tf),
        in_specs=[row,
                  pl.BlockSpec((D_MODEL, tf), lambda i, f: (0, f)),
                  pl.BlockSpec((tf, D_MODEL), lambda i, f: (f, 0)),
                  row,
                  pl.BlockSpec((1, 1, D_MODEL), lambda i, f: (i // per_batch, 0, 0)),
                  vec, vec],
        out_specs=row,
        out_shape=jax.ShapeDtypeStruct((t, D_MODEL), F32),
        scratch_shapes=[pltpu.VMEM((tm, D_MODEL), F32)],
        compiler_params=_cparams(("parallel", "arbitrary")),
        name="mlp",
    )(h2, w1, w2, x1, g2, ln_g, ln_b)


def _block_diag_weights(bb_re, bb_im, c_re, c_im):
    eye = jnp.eye(SSM_CH // SSM_GROUP, dtype=F32)
    ng = SSM_CH // SSM_GROUP

    def b_blocks(bb):
        b4 = bb.reshape(SSM_GB, ng, SSM_GROUP, SSM_STATE)
        return jnp.einsum('bghp,gk->bghkp', b4, eye).reshape(SSM_GB, SSM_CH, SSM_ST)

    def c_blocks(c):
        c4 = c.reshape(SSM_GB, ng, SSM_GROUP, SSM_STATE)
        return jnp.einsum('bghp,gk->bkpgh', c4, eye).reshape(SSM_GB, SSM_ST, SSM_CH)

    wb = jnp.concatenate([b_blocks(bb_re), b_blocks(bb_im)], axis=-1).astype(BF16)
    wc = jnp.concatenate([c_blocks(c_re), -c_blocks(c_im)], axis=1).astype(BF16)
    return wb, wc


def kernel(x, c, w_ada, b_ada, w_in, lam_re, lam_im, log_dt, ssm_b_re, ssm_b_im, ssm_c_re, ssm_c_im, ssm_d, w_glu_val, w_glu_gate, w_pool, pool_scale, w_pool_out, w_out, ln1_g, ln1_b, w_ff1, w_ff2, ln2_g, ln2_b):
    bsz, seq, d = x.shape
    t = bsz * seq
    l = 0
    x2 = x.reshape(t, d)

    c_pad = jnp.zeros((SUBLANES, d), F32).at[:bsz].set(c)
    mod = _mod(c_pad, w_ada[l], b_ada[l][None, :])[:bsz]
    sh1, sc1, g1, sh2, sc2, g2 = [m[:, None, :] for m in jnp.split(mod, N_MOD, axis=-1)]

    proj, u_ssm = _in_proj(x2, sh1, sc1, w_in[l].astype(BF16), seq)

    a_re, a_im, al_re, al_im, bb_re, bb_im = _ssm_prep(
        lam_re[l], lam_im[l], log_dt[l], ssm_b_re[l], ssm_b_im[l])
    wb, wc = _block_diag_weights(bb_re, bb_im, ssm_c_re[l], ssm_c_im[l])

    def state_rows(a):
        a = a[::SSM_GROUP].reshape(SSM_GB, 1, SSM_ST)
        return jnp.broadcast_to(a, (SSM_GB, SUBLANES, SSM_ST))

    z = _ssm(u_ssm, wb, wc, state_rows(a_re), state_rows(a_im), state_rows(al_re),
             state_rows(al_im), ssm_d[l].reshape(SSM_GB, 1, SSM_CH), seq)

    yp = _pool(proj, w_pool[l].astype(BF16), pool_scale[l][None, :], seq)

    merged = _merge(z, yp, proj, w_glu_val[l].astype(BF16), w_glu_gate[l].astype(BF16),
                    w_pool_out[l].astype(BF16))

    x1, h2 = _out_ln(merged, w_out[l].astype(BF16), x2, g1, ln1_g[l][None, :],
                     ln1_b[l][None, :], sh2, sc2, seq)

    out = _mlp(h2, w_ff1[l].astype(BF16), w_ff2[l].astype(BF16), x1, g2,
               ln2_g[l][None, :], ln2_b[l][None, :], seq)
    return out.reshape(bsz, seq, d)
```

```python
import functools
import math

import jax
import jax.numpy as jnp
from jax import lax
from jax.experimental import pallas as pl
from jax.experimental.pallas import tpu as pltpu

D_MODEL = 2048
SSM_WIDTH = 1024
SSM_GROUP = 16
SSM_GROUPS = 64
SSM_STATE = 64
POOL_WIDTH = 1024
POOL_WINDOWS = (2, 4, 8, 16)
POOL_GROUP = 256
IN_WIDTH = SSM_WIDTH + POOL_WIDTH + 2 * D_MODEL
D_FF = 4 * D_MODEL
N_MOD = 6
LN_EPS = 1e-5
ALPHA = 2.0 ** 0.25

SUBLANES = 8
LANES = 128
VMEM_LIMIT = 56 * 1024 * 1024

SSM_CHUNK = 128
SSM_ROWS = SUBLANES * SSM_CHUNK
SSM_GB = 4
SSM_CH = SSM_WIDTH // SSM_GB
SSM_ST = SSM_CH // SSM_GROUP * SSM_STATE
SSM_COLS = 512

BF16 = jnp.bfloat16
F32 = jnp.float32


def _cparams(sem):
    return pltpu.CompilerParams(dimension_semantics=sem, vmem_limit_bytes=VMEM_LIMIT)


def _ln(x):
    mu = jnp.mean(x, axis=-1, keepdims=True)
    xc = x - mu
    var = jnp.mean(xc * xc, axis=-1, keepdims=True)
    return xc * lax.rsqrt(var + LN_EPS)


def _mod_kernel(c_ref, w_ref, b_ref, o_ref):
    c = c_ref[...]
    c_act = (c * jax.nn.sigmoid(c)).astype(BF16)
    o_ref[...] = jnp.dot(c_act, w_ref[...].astype(BF16),
                         preferred_element_type=F32) + b_ref[...]


def _mod(c_pad, w_ada, b_ada):
    tn = 1024
    n = w_ada.shape[1]
    return pl.pallas_call(
        _mod_kernel,
        grid=(n // tn,),
        in_specs=[pl.BlockSpec((SUBLANES, D_MODEL), lambda j: (0, 0)),
                  pl.BlockSpec((D_MODEL, tn), lambda j: (0, j)),
                  pl.BlockSpec((1, tn), lambda j: (0, j))],
        out_specs=pl.BlockSpec((SUBLANES, tn), lambda j: (0, j)),
        out_shape=jax.ShapeDtypeStruct((SUBLANES, n), F32),
        compiler_params=_cparams(("arbitrary",)),
        name="mod",
    )(c_pad, w_ada, b_ada)


def _in_proj_kernel(x_ref, sh_ref, sc_ref, w_ref, proj_ref, ussm_ref, h_ref, *, tn):
    j = pl.program_id(1)

    @pl.when(j == 0)
    def _():
        y = _ln(x_ref[...])
        h_ref[...] = (y * (1.0 + sc_ref[0]) + sh_ref[0]).astype(BF16)

    acc = jnp.dot(h_ref[...], w_ref[...].astype(BF16), preferred_element_type=F32)
    proj_ref[...] = acc.astype(BF16)

    @pl.when(j < SSM_WIDTH // tn)
    def _():
        ussm_ref[...] = acc


def _in_proj(x2, sh, sc, w_in, seq):
    tm, tn = 1024, 512
    t = x2.shape[0]
    per_batch = seq // tm
    nu = SSM_WIDTH // tn
    return pl.pallas_call(
        functools.partial(_in_proj_kernel, tn=tn),
        grid=(t // tm, IN_WIDTH // tn),
        in_specs=[pl.BlockSpec((tm, D_MODEL), lambda i, j: (i, 0)),
                  pl.BlockSpec((1, 1, D_MODEL), lambda i, j: (i // per_batch, 0, 0)),
                  pl.BlockSpec((1, 1, D_MODEL), lambda i, j: (i // per_batch, 0, 0)),
                  pl.BlockSpec((D_MODEL, tn), lambda i, j: (0, j))],
        out_specs=[pl.BlockSpec((tm, tn), lambda i, j: (i, j)),
                   pl.BlockSpec((tm, tn), lambda i, j: (i, jnp.minimum(j, nu - 1)))],
        out_shape=[jax.ShapeDtypeStruct((t, IN_WIDTH), BF16),
                   jax.ShapeDtypeStruct((t, SSM_WIDTH), F32)],
        scratch_shapes=[pltpu.VMEM((tm, D_MODEL), BF16)],
        compiler_params=_cparams(("parallel", "arbitrary")),
        name="in_proj",
    )(x2, sh, sc, w_in)


def _ssm_prep_kernel(lr_ref, li_ref, ldt_ref, br_ref, bi_ref,
                     are_ref, aim_ref, alre_ref, alim_ref, bbre_ref, bbim_ref):
    lr = lr_ref[...]
    li = li_ref[...]
    dt = jnp.exp(ldt_ref[...])
    mag = jnp.exp(lr * dt)
    ang = li * dt
    ab_re = mag * jnp.cos(ang)
    ab_im = mag * jnp.sin(ang)
    are_ref[...] = ab_re
    aim_ref[...] = ab_im
    mag_l = jnp.exp(lr * dt * float(SSM_CHUNK))
    ang_l = ang * float(SSM_CHUNK)
    alre_ref[...] = mag_l * jnp.cos(ang_l)
    alim_ref[...] = mag_l * jnp.sin(ang_l)
    num_re = ab_re - 1.0
    num_im = ab_im
    den = lr * lr + li * li
    f_re = (num_re * lr + num_im * li) / den
    f_im = (num_im * lr - num_re * li) / den
    br = br_ref[...]
    bi = bi_ref[...]
    bbre_ref[...] = f_re * br - f_im * bi
    bbim_ref[...] = f_re * bi + f_im * br


def _ssm_prep(lam_re, lam_im, log_dt, b_re, b_im):
    rep = lambda a: jnp.repeat(a, SSM_GROUP, axis=0)
    lr = rep(lam_re)
    li = rep(lam_im)
    ldt = rep(jnp.broadcast_to(log_dt[:, None], (SSM_GROUPS, SSM_STATE)))
    br = b_re.transpose(0, 2, 1).reshape(SSM_WIDTH, SSM_STATE)
    bi = b_im.transpose(0, 2, 1).reshape(SSM_WIDTH, SSM_STATE)
    shp = jax.ShapeDtypeStruct((SSM_WIDTH, SSM_STATE), F32)
    return pl.pallas_call(
        _ssm_prep_kernel,
        out_shape=[shp] * 6,
        name="ssm_prep",
    )(lr, li, ldt, br, bi)


def _ssm_kernel(u0_ref, u1_ref, wb_ref, wc_ref, are_ref, aim_ref, alre_ref, alim_ref,
                d_ref, z_ref, up_ref, upb_ref, bu_ref, xb_ref, cst_ref, st_ref,
                ys0_ref, ys1_ref):
    k = pl.program_id(2)
    L = SSM_CHUNK

    @pl.when(k == 0)
    def _():
        cst_ref[...] = jnp.zeros_like(cst_ref)

    def permute_in(l, c):
        rows = pl.ds(pl.multiple_of(l * SUBLANES, SUBLANES), SUBLANES)
        up_ref[rows, 0:LANES] = u0_ref[pl.ds(l, SUBLANES, stride=L), :]
        up_ref[rows, LANES:2 * LANES] = u1_ref[pl.ds(l, SUBLANES, stride=L), :]
        return c
    lax.fori_loop(0, L, permute_in, 0)

    upb_ref[...] = up_ref[...].astype(BF16)
    for n in range(2 * SSM_ST // 256):
        bu_ref[:, n * 256:(n + 1) * 256] = jnp.dot(
            upb_ref[...], wb_ref[0, :, n * 256:(n + 1) * 256],
            preferred_element_type=F32)

    row_id = lax.broadcasted_iota(jnp.int32, (SUBLANES, SSM_COLS), 0)
    for cg in range(SSM_ST // SSM_COLS):
        re_sl = slice(cg * SSM_COLS, (cg + 1) * SSM_COLS)
        im_sl = slice(SSM_ST + cg * SSM_COLS, SSM_ST + (cg + 1) * SSM_COLS)
        ar = are_ref[0, :, re_sl]
        ai = aim_ref[0, :, re_sl]

        def step(l, sr, si):
            row = pl.multiple_of(l * SUBLANES, SUBLANES)
            br = bu_ref[pl.ds(row, SUBLANES), re_sl]
            bi = bu_ref[pl.ds(row, SUBLANES), im_sl]
            return ar * sr - ai * si + br, ar * si + ai * sr + bi

        def pass1(l, carry):
            return step(l, *carry)
        er, ei = lax.fori_loop(0, L, pass1, (cst_ref[:, re_sl], cst_ref[:, im_sl]),
                               unroll=2)

        alr = alre_ref[0, :, re_sl]
        ali = alim_ref[0, :, re_sl]
        tr, ti = er, ei
        for r in range(1, SUBLANES):
            pr = pltpu.roll(tr, 1, 0)
            pi = pltpu.roll(ti, 1, 0)
            tr = jnp.where(row_id == r, er + alr * pr - ali * pi, tr)
            ti = jnp.where(row_id == r, ei + alr * pi + ali * pr, ti)
        pr = pltpu.roll(tr, 1, 0)
        pi = pltpu.roll(ti, 1, 0)
        first = row_id == 0
        st_ref[:, re_sl] = jnp.where(first, cst_ref[:, re_sl], pr)
        st_ref[:, im_sl] = jnp.where(first, cst_ref[:, im_sl], pi)
        cst_ref[:, re_sl] = jnp.where(first, pr, 0.0)
        cst_ref[:, im_sl] = jnp.where(first, pi, 0.0)

        def pass2(m, carry):
            sr, si = carry
            r1, i1 = step(2 * m, sr, si)
            r2, i2 = step(2 * m + 1, r1, i1)
            row = pl.multiple_of(m * 2 * SUBLANES, 2 * SUBLANES)
            xb_ref[pl.ds(row, 2 * SUBLANES), re_sl] = (
                jnp.concatenate([r1, r2], axis=0).astype(BF16))
            xb_ref[pl.ds(row, 2 * SUBLANES), im_sl] = (
                jnp.concatenate([i1, i2], axis=0).astype(BF16))
            return r2, i2
        lax.fori_loop(0, L // 2, pass2, (st_ref[:, re_sl], st_ref[:, im_sl]))

    mrows = 256
    for m in range(SSM_ROWS // mrows):
        rs = slice(m * mrows, (m + 1) * mrows)
        y = jnp.dot(xb_ref[rs, :], wc_ref[0], preferred_element_type=F32)
        y = y + d_ref[0] * up_ref[rs, :]
        ys0_ref[rs, :] = y[:, 0:LANES]
        ys1_ref[rs, :] = y[:, LANES:2 * LANES]

    for r in range(SUBLANES):
        for h, ys_ref in enumerate((ys0_ref, ys1_ref)):
            y = ys_ref[pl.ds(r, L, stride=SUBLANES), :]
            z_ref[r * L:(r + 1) * L, h * LANES:(h + 1) * LANES] = (
                jax.nn.gelu(y, approximate=True).astype(BF16))


def _ssm(u, wb, wc, a_re, a_im, al_re, al_im, d, seq):
    t = u.shape[0]
    nb = t // seq
    nk = seq // SSM_ROWS
    tok = lambda g, b, k: (b * nk + k, g)
    par = lambda g, b, k: (g, 0, 0)
    return pl.pallas_call(
        _ssm_kernel,
        grid=(SSM_GB, nb, nk),
        in_specs=[pl.BlockSpec((SSM_ROWS, LANES), lambda g, b, k: (b * nk + k, 2 * g)),
                  pl.BlockSpec((SSM_ROWS, LANES), lambda g, b, k: (b * nk + k, 2 * g + 1)),
                  pl.BlockSpec((1, SSM_CH, 2 * SSM_ST), par),
                  pl.BlockSpec((1, 2 * SSM_ST, SSM_CH), par),
                  pl.BlockSpec((1, SUBLANES, SSM_ST), par),
                  pl.BlockSpec((1, SUBLANES, SSM_ST), par),
                  pl.BlockSpec((1, SUBLANES, SSM_ST), par),
                  pl.BlockSpec((1, SUBLANES, SSM_ST), par),
                  pl.BlockSpec((1, 1, SSM_CH), par)],
        out_specs=pl.BlockSpec((SSM_ROWS, SSM_CH), tok),
        out_shape=jax.ShapeDtypeStruct((t, SSM_WIDTH), BF16),
        scratch_shapes=[pltpu.VMEM((SSM_ROWS, SSM_CH), F32),
                        pltpu.VMEM((SSM_ROWS, SSM_CH), BF16),
                        pltpu.VMEM((SSM_ROWS, 2 * SSM_ST), F32),
                        pltpu.VMEM((SSM_ROWS, 2 * SSM_ST), BF16),
                        pltpu.VMEM((SUBLANES, 2 * SSM_ST), F32),
                        pltpu.VMEM((SUBLANES, 2 * SSM_ST), F32),
                        pltpu.VMEM((SSM_ROWS, LANES), F32),
                        pltpu.VMEM((SSM_ROWS, LANES), F32)],
        compiler_params=_cparams(("parallel", "parallel", "arbitrary")),
        name="ssm",
    )(u, u, wb, wc, a_re, a_im, al_re, al_im, d)


POOL_HALO = 16


def _pool_kernel(u_ref, w_ref, s_ref, o_ref, ext_ref, *, tt):
    k = pl.program_id(1)

    @pl.when(k == 0)
    def _():
        ext_ref[0:POOL_HALO, :] = jnp.zeros((POOL_HALO, POOL_WIDTH), F32)

    @pl.when(k != 0)
    def _():
        ext_ref[0:POOL_HALO, :] = ext_ref[tt:tt + POOL_HALO, :]

    ext_ref[POOL_HALO:POOL_HALO + tt, :] = u_ref[...].astype(F32)
    pos = (k * tt + lax.broadcasted_iota(jnp.int32, (tt, POOL_GROUP), 0)).astype(F32)
    for gi, w in enumerate(POOL_WINDOWS):
        cs = slice(gi * POOL_GROUP, (gi + 1) * POOL_GROUP)
        cur = ext_ref[POOL_HALO:POOL_HALO + tt, cs]
        tot = cur
        for j in range(1, w):
            tot = tot + ext_ref[POOL_HALO - j:POOL_HALO - j + tt, cs]
        count = jnp.minimum(pos + 1.0, float(w))
        pooled = tot / count - cur
        y = jnp.dot(pooled.astype(BF16), w_ref[gi], preferred_element_type=F32)
        o_ref[:, cs] = (y * s_ref[:, cs]).astype(BF16)


def _pool(proj, w_pool, pool_scale, seq):
    tt = 512
    t = proj.shape[0]
    nk = seq // tt
    return pl.pallas_call(
        functools.partial(_pool_kernel, tt=tt),
        grid=(t // seq, nk),
        in_specs=[pl.BlockSpec((tt, POOL_WIDTH), lambda b, k: (b * nk + k, 1)),
                  pl.BlockSpec((len(POOL_WINDOWS), POOL_GROUP, POOL_GROUP),
                               lambda b, k: (0, 0, 0)),
                  pl.BlockSpec((1, POOL_WIDTH), lambda b, k: (0, 0))],
        out_specs=pl.BlockSpec((tt, POOL_WIDTH), lambda b, k: (b * nk + k, 0)),
        out_shape=jax.ShapeDtypeStruct((t, POOL_WIDTH), BF16),
        scratch_shapes=[pltpu.VMEM((tt + POOL_HALO, POOL_WIDTH), F32)],
        compiler_params=_cparams(("parallel", "arbitrary")),
        name="pool",
    )(proj, w_pool, pool_scale)


def _merge_kernel(z_ref, yp_ref, ga_ref, gb_ref, wv_ref, wg_ref, wp_ref, o_ref):
    z = z_ref[...]
    val = jnp.dot(z, wv_ref[...].astype(BF16), preferred_element_type=F32)
    gate = jnp.dot(z, wg_ref[...].astype(BF16), preferred_element_type=F32)
    y_a = val * jax.nn.sigmoid(gate)
    y_b = jnp.dot(yp_ref[...], wp_ref[...].astype(BF16), preferred_element_type=F32)
    merged = (jax.nn.sigmoid(ga_ref[...].astype(F32)) * y_a
              + jax.nn.sigmoid(gb_ref[...].astype(F32)) * y_b)
    o_ref[...] = merged.astype(BF16)


def _merge(z, yp, proj, w_val, w_gate, w_po):
    tm, tn = 1024, 512
    t = z.shape[0]
    ga0 = (SSM_WIDTH + POOL_WIDTH) // tn
    gb0 = ga0 + D_MODEL // tn
    wspec = pl.BlockSpec((SSM_WIDTH, tn), lambda i, j: (0, j))
    return pl.pallas_call(
        _merge_kernel,
        grid=(t // tm, D_MODEL // tn),
        in_specs=[pl.BlockSpec((tm, SSM_WIDTH), lambda i, j: (i, 0)),
                  pl.BlockSpec((tm, POOL_WIDTH), lambda i, j: (i, 0)),
                  pl.BlockSpec((tm, tn), lambda i, j: (i, ga0 + j)),
                  pl.BlockSpec((tm, tn), lambda i, j: (i, gb0 + j)),
                  wspec, wspec, wspec],
        out_specs=pl.BlockSpec((tm, tn), lambda i, j: (i, j)),
        out_shape=jax.ShapeDtypeStruct((t, D_MODEL), BF16),
        compiler_params=_cparams(("parallel", "arbitrary")),
        name="merge",
    )(z, yp, proj, proj, w_val, w_gate, w_po)


def _out_ln_kernel(m_ref, w_ref, x_ref, g1_ref, lg_ref, lb_ref, sh_ref, sc_ref,
                   x1_ref, h2_ref, wbf_ref):
    @pl.when(pl.program_id(0) == 0)
    def _():
        wbf_ref[...] = w_ref[...].astype(BF16)

    y = jnp.dot(m_ref[...], wbf_ref[...], preferred_element_type=F32)
    r = ALPHA * x_ref[...] + g1_ref[0] * y
    x1 = _ln(r) * lg_ref[...] + lb_ref[...]
    x1_ref[...] = x1
    h2_ref[...] = (_ln(x1) * (1.0 + sc_ref[0]) + sh_ref[0]).astype(BF16)


def _out_ln(merged, w_out, x2, g1, ln_g, ln_b, sh2, sc2, seq):
    tm = 512
    t = x2.shape[0]
    per_batch = seq // tm
    row = pl.BlockSpec((tm, D_MODEL), lambda i: (i, 0))
    vec = pl.BlockSpec((1, D_MODEL), lambda i: (0, 0))
    bvec = pl.BlockSpec((1, 1, D_MODEL), lambda i: (i // per_batch, 0, 0))
    return pl.pallas_call(
        _out_ln_kernel,
        grid=(t // tm,),
        in_specs=[row,
                  pl.BlockSpec((D_MODEL, D_MODEL), lambda i: (0, 0),
                               pipeline_mode=pl.Buffered(1)),
                  row, bvec, vec, vec, bvec, bvec],
        out_specs=[row, row],
        out_shape=[jax.ShapeDtypeStruct((t, D_MODEL), F32),
                   jax.ShapeDtypeStruct((t, D_MODEL), BF16)],
        scratch_shapes=[pltpu.VMEM((D_MODEL, D_MODEL), BF16)],
        compiler_params=_cparams(("arbitrary",)),
        name="out_ln",
    )(merged, w_out, x2, g1, ln_g, ln_b, sh2, sc2)


def _mlp_kernel(h_ref, w1_ref, w2_ref, x1_ref, g2_ref, lg_ref, lb_ref, o_ref):
    f = pl.program_id(1)

    @pl.when(f == 0)
    def _():
        o_ref[...] = jnp.zeros_like(o_ref)

    a = jnp.maximum(jnp.dot(h_ref[...], w1_ref[...].astype(BF16),
                            preferred_element_type=F32), 0.0)
    o_ref[...] += jnp.dot((a * a).astype(BF16), w2_ref[...].astype(BF16),
                          preferred_element_type=F32)

    @pl.when(f == pl.num_programs(1) - 1)
    def _():
        r = ALPHA * x1_ref[...] + g2_ref[0] * o_ref[...]
        o_ref[...] = _ln(r) * lg_ref[...] + lb_ref[...]


def _mlp(h2, w1, w2, x1, g2, ln_g, ln_b, seq):
    tm, tf = 1024, 512
    t = h2.shape[0]
    per_batch = seq // tm
    once = pl.Buffered(1)
    vec = pl.BlockSpec((1, D_MODEL), lambda i, f: (0, 0))
    return pl.pallas_call(
        _mlp_kernel,
        grid=(t // tm, D_FF // tf),
        in_specs=[pl.BlockSpec((tm, D_MODEL), lambda i, f: (i, 0), pipeline_mode=once),
                  pl.BlockSpec((D_MODEL, tf), lambda i, f: (0, f)),
                  pl.BlockSpec((tf, D_MODEL), lambda i, f: (f, 0)),
                  pl.BlockSpec((tm, D_MODEL), lambda i, f: (i, 0), pipeline_mode=once),
                  pl.BlockSpec((1, 1, D_MODEL), lambda i, f: (i // per_batch, 0, 0)),
                  vec, vec],
        out_specs=pl.BlockSpec((tm, D_MODEL), lambda i, f: (i, 0)),
        out_shape=jax.ShapeDtypeStruct((t, D_MODEL), F32),
        compiler_params=_cparams(("parallel", "arbitrary")),
        name="mlp",
    )(h2, w1, w2, x1, g2, ln_g, ln_b)


def _block_diag_weights(bb_re, bb_im, c_re, c_im):
    eye = jnp.eye(SSM_CH // SSM_GROUP, dtype=F32)
    ng = SSM_CH // SSM_GROUP

    def b_blocks(bb):
        b4 = bb.reshape(SSM_GB, ng, SSM_GROUP, SSM_STATE)
        return jnp.einsum('bghp,gk->bghkp', b4, eye).reshape(SSM_GB, SSM_CH, SSM_ST)

    def c_blocks(c):
        c4 = c.reshape(SSM_GB, ng, SSM_GROUP, SSM_STATE)
        return jnp.einsum('bghp,gk->bkpgh', c4, eye).reshape(SSM_GB, SSM_ST, SSM_CH)

    wb = jnp.concatenate([b_blocks(bb_re), b_blocks(bb_im)], axis=-1).astype(BF16)
    wc = jnp.concatenate([c_blocks(c_re), -c_blocks(c_im)], axis=1).astype(BF16)
    return wb, wc


def kernel(x, c, w_ada, b_ada, w_in, lam_re, lam_im, log_dt, ssm_b_re, ssm_b_im, ssm_c_re, ssm_c_im, ssm_d, w_glu_val, w_glu_gate, w_pool, pool_scale, w_pool_out, w_out, ln1_g, ln1_b, w_ff1, w_ff2, ln2_g, ln2_b):
    bsz, seq, d = x.shape
    t = bsz * seq
    l = 0
    x2 = x.reshape(t, d)

    c_pad = jnp.zeros((SUBLANES, d), F32).at[:bsz].set(c)
    mod = _mod(c_pad, w_ada[l], b_ada[l][None, :])[:bsz]
    sh1, sc1, g1, sh2, sc2, g2 = [m[:, None, :] for m in jnp.split(mod, N_MOD, axis=-1)]

    proj, u_ssm = _in_proj(x2, sh1, sc1, w_in[l], seq)

    a_re, a_im, al_re, al_im, bb_re, bb_im = _ssm_prep(
        lam_re[l], lam_im[l], log_dt[l], ssm_b_re[l], ssm_b_im[l])
    wb, wc = _block_diag_weights(bb_re, bb_im, ssm_c_re[l], ssm_c_im[l])

    def state_rows(a):
        a = a[::SSM_GROUP].reshape(SSM_GB, 1, SSM_ST)
        return jnp.broadcast_to(a, (SSM_GB, SUBLANES, SSM_ST))

    z = _ssm(u_ssm, wb, wc, state_rows(a_re), state_rows(a_im), state_rows(al_re),
             state_rows(al_im), ssm_d[l].reshape(SSM_GB, 1, SSM_CH), seq)

    yp = _pool(proj, w_pool[l].astype(BF16), pool_scale[l][None, :], seq)

    merged = _merge(z, yp, proj, w_glu_val[l], w_glu_gate[l], w_pool_out[l])

    x1, h2 = _out_ln(merged, w_out[l], x2, g1, ln1_g[l][None, :],
                     ln1_b[l][None, :], sh2, sc2, seq)

    out = _mlp(h2, w_ff1[l], w_ff2[l], x1, g2,
               ln2_g[l][None, :], ln2_b[l][None, :], seq)
    return out.reshape(bsz, seq, d)
```

```python
import functools
import math

import jax
import jax.numpy as jnp
from jax import lax
from jax.experimental import pallas as pl
from jax.experimental.pallas import tpu as pltpu

D_MODEL = 2048
SSM_WIDTH = 1024
SSM_GROUP = 16
SSM_GROUPS = 64
SSM_STATE = 64
POOL_WIDTH = 1024
POOL_WINDOWS = (2, 4, 8, 16)
POOL_GROUP = 256
IN_WIDTH = SSM_WIDTH + POOL_WIDTH + 2 * D_MODEL
D_FF = 4 * D_MODEL
N_MOD = 6
LN_EPS = 1e-5
ALPHA = 2.0 ** 0.25

SUBLANES = 8
LANES = 128
VMEM_LIMIT = 56 * 1024 * 1024

SSM_CHUNK = 128
SSM_ROWS = SUBLANES * SSM_CHUNK
SSM_GB = 4
SSM_CH = SSM_WIDTH // SSM_GB
SSM_ST = SSM_CH // SSM_GROUP * SSM_STATE
SSM_COLS = 512

BF16 = jnp.bfloat16
F32 = jnp.float32


def _cparams(sem):
    return pltpu.CompilerParams(dimension_semantics=sem, vmem_limit_bytes=VMEM_LIMIT)


def _ln(x):
    mu = jnp.mean(x, axis=-1, keepdims=True)
    xc = x - mu
    var = jnp.mean(xc * xc, axis=-1, keepdims=True)
    return xc * lax.rsqrt(var + LN_EPS)


def _mod_kernel(c_ref, w_ref, b_ref, o_ref):
    c = c_ref[...]
    c_act = (c * jax.nn.sigmoid(c)).astype(BF16)
    o_ref[...] = jnp.dot(c_act, w_ref[...].astype(BF16),
                         preferred_element_type=F32) + b_ref[...]


def _mod(c_pad, w_ada, b_ada):
    tn = 1024
    n = w_ada.shape[1]
    return pl.pallas_call(
        _mod_kernel,
        grid=(n // tn,),
        in_specs=[pl.BlockSpec((SUBLANES, D_MODEL), lambda j: (0, 0)),
                  pl.BlockSpec((D_MODEL, tn), lambda j: (0, j)),
                  pl.BlockSpec((1, tn), lambda j: (0, j))],
        out_specs=pl.BlockSpec((SUBLANES, tn), lambda j: (0, j)),
        out_shape=jax.ShapeDtypeStruct((SUBLANES, n), F32),
        compiler_params=_cparams(("arbitrary",)),
        name="mod",
    )(c_pad, w_ada, b_ada)


def _in_proj_kernel(x_ref, sh_ref, sc_ref, w_ref, proj_ref, h_ref):
    j = pl.program_id(1)

    @pl.when(j == 0)
    def _():
        y = _ln(x_ref[...])
        h_ref[...] = (y * (1.0 + sc_ref[0]) + sh_ref[0]).astype(BF16)

    acc = jnp.dot(h_ref[...], w_ref[...], preferred_element_type=F32)
    proj_ref[...] = acc.astype(BF16)


def _in_proj(x2, sh, sc, w_in, seq):
    tm, tn = 1024, 1024
    t = x2.shape[0]
    per_batch = seq // tm
    return pl.pallas_call(
        _in_proj_kernel,
        grid=(t // tm, IN_WIDTH // tn),
        in_specs=[pl.BlockSpec((tm, D_MODEL), lambda i, j: (i, 0)),
                  pl.BlockSpec((1, 1, D_MODEL), lambda i, j: (i // per_batch, 0, 0)),
                  pl.BlockSpec((1, 1, D_MODEL), lambda i, j: (i // per_batch, 0, 0)),
                  pl.BlockSpec((D_MODEL, tn), lambda i, j: (0, j))],
        out_specs=pl.BlockSpec((tm, tn), lambda i, j: (i, j)),
        out_shape=jax.ShapeDtypeStruct((t, IN_WIDTH), BF16),
        scratch_shapes=[pltpu.VMEM((tm, D_MODEL), BF16)],
        compiler_params=_cparams(("parallel", "arbitrary")),
        name="in_proj",
    )(x2, sh, sc, w_in)


def _ssm_prep_kernel(lr_ref, li_ref, ldt_ref, br_ref, bi_ref,
                     are_ref, aim_ref, alre_ref, alim_ref, bbre_ref, bbim_ref):
    lr = lr_ref[...]
    li = li_ref[...]
    dt = jnp.exp(ldt_ref[...])
    mag = jnp.exp(lr * dt)
    ang = li * dt
    ab_re = mag * jnp.cos(ang)
    ab_im = mag * jnp.sin(ang)
    are_ref[...] = ab_re
    aim_ref[...] = ab_im
    mag_l = jnp.exp(lr * dt * float(SSM_CHUNK))
    ang_l = ang * float(SSM_CHUNK)
    alre_ref[...] = mag_l * jnp.cos(ang_l)
    alim_ref[...] = mag_l * jnp.sin(ang_l)
    num_re = ab_re - 1.0
    num_im = ab_im
    den = lr * lr + li * li
    f_re = (num_re * lr + num_im * li) / den
    f_im = (num_im * lr - num_re * li) / den
    br = br_ref[...]
    bi = bi_ref[...]
    bbre_ref[...] = f_re * br - f_im * bi
    bbim_ref[...] = f_re * bi + f_im * br


def _ssm_prep(lam_re, lam_im, log_dt, b_re, b_im):
    rep = lambda a: jnp.repeat(a, SSM_GROUP, axis=0)
    lr = rep(lam_re)
    li = rep(lam_im)
    ldt = rep(jnp.broadcast_to(log_dt[:, None], (SSM_GROUPS, SSM_STATE)))
    br = b_re.transpose(0, 2, 1).reshape(SSM_WIDTH, SSM_STATE)
    bi = b_im.transpose(0, 2, 1).reshape(SSM_WIDTH, SSM_STATE)
    shp = jax.ShapeDtypeStruct((SSM_WIDTH, SSM_STATE), F32)
    return pl.pallas_call(
        _ssm_prep_kernel,
        out_shape=[shp] * 6,
        name="ssm_prep",
    )(lr, li, ldt, br, bi)


def _sublane_transpose(blocks):
    sub = lax.broadcasted_iota(jnp.int32, blocks[0].shape, 1)
    a = list(blocks)
    for s in (1, 2, 4):
        keep = (sub & s) == 0
        nxt = list(a)
        for r in range(SUBLANES):
            if r & s == 0:
                lo, hi = a[r], a[r + s]
                nxt[r] = jnp.where(keep, lo, pltpu.roll(hi, s, 1))
                nxt[r + s] = jnp.where(keep, pltpu.roll(lo, SUBLANES - s, 1), hi)
        a = nxt
    return a


def _ssm_kernel(u_ref, wb_ref, wc_ref, are_ref, aim_ref, alre_ref,
                alim_ref, d_ref, w1_ref, w2_ref, z_ref, w1o_ref, w2o_ref,
                up_ref, upb_ref, bu_ref, xb_ref, cst_ref, st_ref, zp_ref):
    k = pl.program_id(2)
    L = SSM_CHUNK
    lh = L // SUBLANES

    w1o_ref[...] = w1_ref[...].astype(BF16)
    w2o_ref[...] = w2_ref[...].astype(BF16)

    @pl.when(k == 0)
    def _():
        cst_ref[...] = jnp.zeros_like(cst_ref)

    u4 = u_ref[...].astype(F32).reshape(SUBLANES, lh, SUBLANES, SSM_CH)
    for l_lo, blk in enumerate(_sublane_transpose([u4[r] for r in range(SUBLANES)])):
        up_ref[:, l_lo] = blk
    upb_ref[...] = up_ref[...].reshape(SSM_ROWS, SSM_CH).astype(BF16)

    for n in range(2 * SSM_ST // 256):
        bu_ref[:, n * 256:(n + 1) * 256] = jnp.dot(
            upb_ref[...], wb_ref[0, :, n * 256:(n + 1) * 256],
            preferred_element_type=F32)

    row_id = lax.broadcasted_iota(jnp.int32, (SUBLANES, SSM_COLS), 0)
    for cg in range(SSM_ST // SSM_COLS):
        re_sl = slice(cg * SSM_COLS, (cg + 1) * SSM_COLS)
        im_sl = slice(SSM_ST + cg * SSM_COLS, SSM_ST + (cg + 1) * SSM_COLS)
        ar = are_ref[0, :, re_sl]
        ai = aim_ref[0, :, re_sl]

        def step(l, sr, si):
            row = pl.multiple_of(l * SUBLANES, SUBLANES)
            br = bu_ref[pl.ds(row, SUBLANES), re_sl]
            bi = bu_ref[pl.ds(row, SUBLANES), im_sl]
            return ar * sr - ai * si + br, ar * si + ai * sr + bi

        def pass1(l, carry):
            return step(l, *carry)
        er, ei = lax.fori_loop(0, L, pass1, (cst_ref[:, re_sl], cst_ref[:, im_sl]),
                               unroll=2)

        alr = alre_ref[0, :, re_sl]
        ali = alim_ref[0, :, re_sl]
        tr, ti = er, ei
        for r in range(1, SUBLANES):
            pr = pltpu.roll(tr, 1, 0)
            pi = pltpu.roll(ti, 1, 0)
            tr = jnp.where(row_id == r, er + alr * pr - ali * pi, tr)
            ti = jnp.where(row_id == r, ei + alr * pi + ali * pr, ti)
        pr = pltpu.roll(tr, 1, 0)
        pi = pltpu.roll(ti, 1, 0)
        first = row_id == 0
        st_ref[:, re_sl] = jnp.where(first, cst_ref[:, re_sl], pr)
        st_ref[:, im_sl] = jnp.where(first, cst_ref[:, im_sl], pi)
        cst_ref[:, re_sl] = jnp.where(first, pr, 0.0)
        cst_ref[:, im_sl] = jnp.where(first, pi, 0.0)

        def pass2(m, carry):
            sr, si = carry
            r1, i1 = step(2 * m, sr, si)
            r2, i2 = step(2 * m + 1, r1, i1)
            row = pl.multiple_of(m * 2 * SUBLANES, 2 * SUBLANES)
            xb_ref[pl.ds(row, 2 * SUBLANES), re_sl] = (
                jnp.concatenate([r1, r2], axis=0).astype(BF16))
            xb_ref[pl.ds(row, 2 * SUBLANES), im_sl] = (
                jnp.concatenate([i1, i2], axis=0).astype(BF16))
            return r2, i2
        lax.fori_loop(0, L // 2, pass2, (st_ref[:, re_sl], st_ref[:, im_sl]))

    mrows = 256
    mh = mrows // (SUBLANES * SUBLANES)
    for m in range(SSM_ROWS // mrows):
        rs = slice(m * mrows, (m + 1) * mrows)
        y = jnp.dot(xb_ref[rs, :], wc_ref[0], preferred_element_type=F32)
        y = y + d_ref[0] * up_ref[m * mh:(m + 1) * mh].reshape(mrows, SSM_CH)
        zp_ref[m * mh:(m + 1) * mh] = jax.nn.gelu(y, approximate=True).reshape(
            mh, SUBLANES, SUBLANES, SSM_CH)

    blocks = _sublane_transpose([zp_ref[:, l_lo] for l_lo in range(SUBLANES)])
    for r, blk in enumerate(blocks):
        z_ref[r * L:(r + 1) * L, :] = blk.reshape(L, SSM_CH).astype(BF16)


def _ssm(proj, wb, wc, a_re, a_im, al_re, al_im, d, w_ff1, w_ff2, seq):
    t = proj.shape[0]
    nb = t // seq
    nk = seq // SSM_ROWS
    steps = SSM_GB * nb * nk
    r1 = D_MODEL // steps
    r2 = D_FF // steps
    tok = lambda g, b, k: (b * nk + k, g)
    par = lambda g, b, k: (g, 0, 0)
    step = lambda g, b, k: ((g * nb + b) * nk + k, 0)
    lh = SSM_CHUNK // SUBLANES
    return pl.pallas_call(
        _ssm_kernel,
        grid=(SSM_GB, nb, nk),
        in_specs=[pl.BlockSpec((SSM_ROWS, SSM_CH), tok),
                  pl.BlockSpec((1, SSM_CH, 2 * SSM_ST), par),
                  pl.BlockSpec((1, 2 * SSM_ST, SSM_CH), par),
                  pl.BlockSpec((1, SUBLANES, SSM_ST), par),
                  pl.BlockSpec((1, SUBLANES, SSM_ST), par),
                  pl.BlockSpec((1, SUBLANES, SSM_ST), par),
                  pl.BlockSpec((1, SUBLANES, SSM_ST), par),
                  pl.BlockSpec((1, 1, SSM_CH), par),
                  pl.BlockSpec((r1, D_FF), step),
                  pl.BlockSpec((r2, D_MODEL), step)],
        out_specs=[pl.BlockSpec((SSM_ROWS, SSM_CH), tok),
                   pl.BlockSpec((r1, D_FF), step),
                   pl.BlockSpec((r2, D_MODEL), step)],
        out_shape=[jax.ShapeDtypeStruct((t, SSM_WIDTH), BF16),
                   jax.ShapeDtypeStruct((D_MODEL, D_FF), BF16),
                   jax.ShapeDtypeStruct((D_FF, D_MODEL), BF16)],
        scratch_shapes=[pltpu.VMEM((lh, SUBLANES, SUBLANES, SSM_CH), F32),
                        pltpu.VMEM((SSM_ROWS, SSM_CH), BF16),
                        pltpu.VMEM((SSM_ROWS, 2 * SSM_ST), F32),
                        pltpu.VMEM((SSM_ROWS, 2 * SSM_ST), BF16),
                        pltpu.VMEM((SUBLANES, 2 * SSM_ST), F32),
                        pltpu.VMEM((SUBLANES, 2 * SSM_ST), F32),
                        pltpu.VMEM((lh, SUBLANES, SUBLANES, SSM_CH), F32)],
        compiler_params=_cparams(("arbitrary", "arbitrary", "arbitrary")),
        name="ssm",
    )(proj, wb, wc, a_re, a_im, al_re, al_im, d, w_ff1, w_ff2)


POOL_HALO = 16


def _pool_kernel(u_ref, w_ref, s_ref, o_ref, ext_ref, *, tt):
    k = pl.program_id(1)

    @pl.when(k == 0)
    def _():
        ext_ref[0:POOL_HALO, :] = jnp.zeros((POOL_HALO, POOL_WIDTH), F32)

    @pl.when(k != 0)
    def _():
        ext_ref[0:POOL_HALO, :] = ext_ref[tt:tt + POOL_HALO, :]

    ext_ref[POOL_HALO:POOL_HALO + tt, :] = u_ref[...].astype(F32)
    pos = (k * tt + lax.broadcasted_iota(jnp.int32, (tt, POOL_GROUP), 0)).astype(F32)
    for gi, w in enumerate(POOL_WINDOWS):
        cs = slice(gi * POOL_GROUP, (gi + 1) * POOL_GROUP)
        cur = ext_ref[POOL_HALO:POOL_HALO + tt, cs]
        tot = cur
        for j in range(1, w):
            tot = tot + ext_ref[POOL_HALO - j:POOL_HALO - j + tt, cs]
        count = jnp.minimum(pos + 1.0, float(w))
        pooled = tot / count - cur
        y = jnp.dot(pooled.astype(BF16), w_ref[gi], preferred_element_type=F32)
        o_ref[:, cs] = (y * s_ref[:, cs]).astype(BF16)


def _pool(proj, w_pool, pool_scale, seq):
    tt = 512
    t = proj.shape[0]
    nk = seq // tt
    return pl.pallas_call(
        functools.partial(_pool_kernel, tt=tt),
        grid=(t // seq, nk),
        in_specs=[pl.BlockSpec((tt, POOL_WIDTH), lambda b, k: (b * nk + k, 1)),
                  pl.BlockSpec((len(POOL_WINDOWS), POOL_GROUP, POOL_GROUP),
                               lambda b, k: (0, 0, 0)),
                  pl.BlockSpec((1, POOL_WIDTH), lambda b, k: (0, 0))],
        out_specs=pl.BlockSpec((tt, POOL_WIDTH), lambda b, k: (b * nk + k, 0)),
        out_shape=jax.ShapeDtypeStruct((t, POOL_WIDTH), BF16),
        scratch_shapes=[pltpu.VMEM((tt + POOL_HALO, POOL_WIDTH), F32)],
        compiler_params=_cparams(("parallel", "arbitrary")),
        name="pool",
    )(proj, w_pool, pool_scale)


def _merge_kernel(z_ref, yp_ref, ga_ref, gb_ref, wv_ref, wg_ref, wp_ref, o_ref):
    z = z_ref[...]
    val = jnp.dot(z, wv_ref[...].astype(BF16), preferred_element_type=F32)
    gate = jnp.dot(z, wg_ref[...].astype(BF16), preferred_element_type=F32)
    y_a = val * jax.nn.sigmoid(gate)
    y_b = jnp.dot(yp_ref[...], wp_ref[...].astype(BF16), preferred_element_type=F32)
    merged = (jax.nn.sigmoid(ga_ref[...].astype(F32)) * y_a
              + jax.nn.sigmoid(gb_ref[...].astype(F32)) * y_b)
    o_ref[...] = merged.astype(BF16)


def _merge(z, yp, proj, w_val, w_gate, w_po):
    tm, tn = 1024, 512
    t = z.shape[0]
    ga0 = (SSM_WIDTH + POOL_WIDTH) // tn
    gb0 = ga0 + D_MODEL // tn
    wspec = pl.BlockSpec((SSM_WIDTH, tn), lambda i, j: (0, j))
    return pl.pallas_call(
        _merge_kernel,
        grid=(t // tm, D_MODEL // tn),
        in_specs=[pl.BlockSpec((tm, SSM_WIDTH), lambda i, j: (i, 0)),
                  pl.BlockSpec((tm, POOL_WIDTH), lambda i, j: (i, 0)),
                  pl.BlockSpec((tm, tn), lambda i, j: (i, ga0 + j)),
                  pl.BlockSpec((tm, tn), lambda i, j: (i, gb0 + j)),
                  wspec, wspec, wspec],
        out_specs=pl.BlockSpec((tm, tn), lambda i, j: (i, j)),
        out_shape=jax.ShapeDtypeStruct((t, D_MODEL), BF16),
        compiler_params=_cparams(("parallel", "arbitrary")),
        name="merge",
    )(z, yp, proj, proj, w_val, w_gate, w_po)


def _out_ln_kernel(m_ref, w_ref, x_ref, g1_ref, lg_ref, lb_ref, sh_ref, sc_ref,
                   x1_ref, h2_ref, wbf_ref):
    @pl.when(pl.program_id(0) == 0)
    def _():
        wbf_ref[...] = w_ref[...].astype(BF16)

    y = jnp.dot(m_ref[...], wbf_ref[...], preferred_element_type=F32)
    r = ALPHA * x_ref[...] + g1_ref[0] * y
    x1 = _ln(r) * lg_ref[...] + lb_ref[...]
    x1_ref[...] = x1
    h2_ref[...] = (_ln(x1) * (1.0 + sc_ref[0]) + sh_ref[0]).astype(BF16)


def _out_ln(merged, w_out, x2, g1, ln_g, ln_b, sh2, sc2, seq):
    tm = 512
    t = x2.shape[0]
    per_batch = seq // tm
    row = pl.BlockSpec((tm, D_MODEL), lambda i: (i, 0))
    vec = pl.BlockSpec((1, D_MODEL), lambda i: (0, 0))
    bvec = pl.BlockSpec((1, 1, D_MODEL), lambda i: (i // per_batch, 0, 0))
    return pl.pallas_call(
        _out_ln_kernel,
        grid=(t // tm,),
        in_specs=[row,
                  pl.BlockSpec((D_MODEL, D_MODEL), lambda i: (0, 0),
                               pipeline_mode=pl.Buffered(1)),
                  row, bvec, vec, vec, bvec, bvec],
        out_specs=[row, row],
        out_shape=[jax.ShapeDtypeStruct((t, D_MODEL), F32),
                   jax.ShapeDtypeStruct((t, D_MODEL), BF16)],
        scratch_shapes=[pltpu.VMEM((D_MODEL, D_MODEL), BF16)],
        compiler_params=_cparams(("arbitrary",)),
        name="out_ln",
    )(merged, w_out, x2, g1, ln_g, ln_b, sh2, sc2)


def _mlp_kernel(h_ref, w1_ref, w2_ref, x1_ref, g2_ref, lg_ref, lb_ref, o_ref):
    f = pl.program_id(1)

    @pl.when(f == 0)
    def _():
        o_ref[...] = jnp.zeros_like(o_ref)

    a = jnp.maximum(jnp.dot(h_ref[...], w1_ref[...], preferred_element_type=F32), 0.0)
    o_ref[...] += jnp.dot((a * a).astype(BF16), w2_ref[...], preferred_element_type=F32)

    @pl.when(f == pl.num_programs(1) - 1)
    def _():
        r = ALPHA * x1_ref[...] + g2_ref[0] * o_ref[...]
        o_ref[...] = _ln(r) * lg_ref[...] + lb_ref[...]


def _mlp(h2, w1, w2, x1, g2, ln_g, ln_b, seq):
    tm, tf = 1024, 512
    t = h2.shape[0]
    per_batch = seq // tm
    once = pl.Buffered(1)
    vec = pl.BlockSpec((1, D_MODEL), lambda i, f: (0, 0))
    return pl.pallas_call(
        _mlp_kernel,
        grid=(t // tm, D_FF // tf),
        in_specs=[pl.BlockSpec((tm, D_MODEL), lambda i, f: (i, 0), pipeline_mode=once),
                  pl.BlockSpec((D_MODEL, tf), lambda i, f: (0, f)),
                  pl.BlockSpec((tf, D_MODEL), lambda i, f: (f, 0)),
                  pl.BlockSpec((tm, D_MODEL), lambda i, f: (i, 0), pipeline_mode=once),
                  pl.BlockSpec((1, 1, D_MODEL), lambda i, f: (i // per_batch, 0, 0)),
                  vec, vec],
        out_specs=pl.BlockSpec((tm, D_MODEL), lambda i, f: (i, 0)),
        out_shape=jax.ShapeDtypeStruct((t, D_MODEL), F32),
        compiler_params=_cparams(("parallel", "arbitrary")),
        name="mlp",
    )(h2, w1, w2, x1, g2, ln_g, ln_b)


def _block_diag_weights(bb_re, bb_im, c_re, c_im):
    eye = jnp.eye(SSM_CH // SSM_GROUP, dtype=F32)
    ng = SSM_CH // SSM_GROUP

    def b_blocks(bb):
        b4 = bb.reshape(SSM_GB, ng, SSM_GROUP, SSM_STATE)
        return jnp.einsum('bghp,gk->bghkp', b4, eye).reshape(SSM_GB, SSM_CH, SSM_ST)

    def c_blocks(c):
        c4 = c.reshape(SSM_GB, ng, SSM_GROUP, SSM_STATE)
        return jnp.einsum('bghp,gk->bkpgh', c4, eye).reshape(SSM_GB, SSM_ST, SSM_CH)

    wb = jnp.concatenate([b_blocks(bb_re), b_blocks(bb_im)], axis=-1).astype(BF16)
    wc = jnp.concatenate([c_blocks(c_re), -c_blocks(c_im)], axis=1).astype(BF16)
    return wb, wc


def kernel(x, c, w_ada, b_ada, w_in, lam_re, lam_im, log_dt, ssm_b_re, ssm_b_im, ssm_c_re, ssm_c_im, ssm_d, w_glu_val, w_glu_gate, w_pool, pool_scale, w_pool_out, w_out, ln1_g, ln1_b, w_ff1, w_ff2, ln2_g, ln2_b):
    bsz, seq, d = x.shape
    t = bsz * seq
    l = 0
    x2 = x.reshape(t, d)

    c_pad = jnp.zeros((SUBLANES, d), F32).at[:bsz].set(c)
    mod = _mod(c_pad, w_ada[l], b_ada[l][None, :])[:bsz]
    sh1, sc1, g1, sh2, sc2, g2 = [m[:, None, :] for m in jnp.split(mod, N_MOD, axis=-1)]

    proj = _in_proj(x2, sh1, sc1, w_in[l].astype(BF16), seq)

    a_re, a_im, al_re, al_im, bb_re, bb_im = _ssm_prep(
        lam_re[l], lam_im[l], log_dt[l], ssm_b_re[l], ssm_b_im[l])
    wb, wc = _block_diag_weights(bb_re, bb_im, ssm_c_re[l], ssm_c_im[l])

    def state_rows(a):
        a = a[::SSM_GROUP].reshape(SSM_GB, 1, SSM_ST)
        return jnp.broadcast_to(a, (SSM_GB, SUBLANES, SSM_ST))

    z, w1_bf, w2_bf = _ssm(proj, wb, wc, state_rows(a_re), state_rows(a_im),
                           state_rows(al_re), state_rows(al_im),
                           ssm_d[l].reshape(SSM_GB, 1, SSM_CH), w_ff1[l], w_ff2[l], seq)

    yp = _pool(proj, w_pool[l].astype(BF16), pool_scale[l][None, :], seq)

    merged = _merge(z, yp, proj, w_glu_val[l], w_glu_gate[l], w_pool_out[l])

    x1, h2 = _out_ln(merged, w_out[l], x2, g1, ln1_g[l][None, :],
                     ln1_b[l][None, :], sh2, sc2, seq)

    out = _mlp(h2, w1_bf, w2_bf, x1, g2,
               ln2_g[l][None, :], ln2_b[l][None, :], seq)
    return out.reshape(bsz, seq, d)
```

```python
import functools
import math

import jax
import jax.numpy as jnp
from jax import lax
from jax.experimental import pallas as pl
from jax.experimental.pallas import tpu as pltpu

D_MODEL = 2048
SSM_WIDTH = 1024
SSM_GROUP = 16
SSM_GROUPS = 64
SSM_STATE = 64
POOL_WIDTH = 1024
POOL_WINDOWS = (2, 4, 8, 16)
POOL_GROUP = 256
IN_WIDTH = SSM_WIDTH + POOL_WIDTH + 2 * D_MODEL
D_FF = 4 * D_MODEL
N_MOD = 6
LN_EPS = 1e-5
ALPHA = 2.0 ** 0.25

SUBLANES = 8
LANES = 128
VMEM_LIMIT = 56 * 1024 * 1024

SSM_CHUNK = 128
SSM_ROWS = SUBLANES * SSM_CHUNK
SSM_GB = 4
SSM_CH = SSM_WIDTH // SSM_GB
SSM_ST = SSM_CH // SSM_GROUP * SSM_STATE
SSM_COLS = 512

BF16 = jnp.bfloat16
F32 = jnp.float32


def _cparams(sem):
    return pltpu.CompilerParams(dimension_semantics=sem, vmem_limit_bytes=VMEM_LIMIT)


def _ln(x):
    mu = jnp.mean(x, axis=-1, keepdims=True)
    xc = x - mu
    var = jnp.mean(xc * xc, axis=-1, keepdims=True)
    return xc * lax.rsqrt(var + LN_EPS)


def _mod_kernel(c_ref, w_ref, b_ref, o_ref):
    c = c_ref[...]
    c_act = (c * jax.nn.sigmoid(c)).astype(BF16)
    o_ref[...] = jnp.dot(c_act, w_ref[...].astype(BF16),
                         preferred_element_type=F32) + b_ref[...]


def _mod(c_pad, w_ada, b_ada):
    tn = 1024
    n = w_ada.shape[1]
    return pl.pallas_call(
        _mod_kernel,
        grid=(n // tn,),
        in_specs=[pl.BlockSpec((SUBLANES, D_MODEL), lambda j: (0, 0)),
                  pl.BlockSpec((D_MODEL, tn), lambda j: (0, j)),
                  pl.BlockSpec((1, tn), lambda j: (0, j))],
        out_specs=pl.BlockSpec((SUBLANES, tn), lambda j: (0, j)),
        out_shape=jax.ShapeDtypeStruct((SUBLANES, n), F32),
        compiler_params=_cparams(("arbitrary",)),
        name="mod",
    )(c_pad, w_ada, b_ada)


def _in_proj_kernel(x_ref, sh_ref, sc_ref, w_ref, proj_ref, h_ref):
    j = pl.program_id(1)

    @pl.when(j == 0)
    def _():
        y = _ln(x_ref[...])
        h_ref[...] = (y * (1.0 + sc_ref[0]) + sh_ref[0]).astype(BF16)

    acc = jnp.dot(h_ref[...], w_ref[...], preferred_element_type=F32)
    proj_ref[...] = acc.astype(BF16)


def _in_proj(x2, sh, sc, w_in, seq):
    tm, tn = 1024, 1024
    t = x2.shape[0]
    per_batch = seq // tm
    return pl.pallas_call(
        _in_proj_kernel,
        grid=(t // tm, IN_WIDTH // tn),
        in_specs=[pl.BlockSpec((tm, D_MODEL), lambda i, j: (i, 0)),
                  pl.BlockSpec((1, 1, D_MODEL), lambda i, j: (i // per_batch, 0, 0)),
                  pl.BlockSpec((1, 1, D_MODEL), lambda i, j: (i // per_batch, 0, 0)),
                  pl.BlockSpec((D_MODEL, tn), lambda i, j: (0, j))],
        out_specs=pl.BlockSpec((tm, tn), lambda i, j: (i, j)),
        out_shape=jax.ShapeDtypeStruct((t, IN_WIDTH), BF16),
        scratch_shapes=[pltpu.VMEM((tm, D_MODEL), BF16)],
        compiler_params=_cparams(("parallel", "arbitrary")),
        name="in_proj",
    )(x2, sh, sc, w_in)


def _ssm_prep_kernel(lr_ref, li_ref, ldt_ref, br_ref, bi_ref,
                     are_ref, aim_ref, alre_ref, alim_ref, bbre_ref, bbim_ref):
    lr = lr_ref[...]
    li = li_ref[...]
    dt = jnp.exp(ldt_ref[...])
    mag = jnp.exp(lr * dt)
    ang = li * dt
    ab_re = mag * jnp.cos(ang)
    ab_im = mag * jnp.sin(ang)
    are_ref[...] = ab_re
    aim_ref[...] = ab_im
    mag_l = jnp.exp(lr * dt * float(SSM_CHUNK))
    ang_l = ang * float(SSM_CHUNK)
    alre_ref[...] = mag_l * jnp.cos(ang_l)
    alim_ref[...] = mag_l * jnp.sin(ang_l)
    num_re = ab_re - 1.0
    num_im = ab_im
    den = lr * lr + li * li
    f_re = (num_re * lr + num_im * li) / den
    f_im = (num_im * lr - num_re * li) / den
    br = br_ref[...]
    bi = bi_ref[...]
    bbre_ref[...] = f_re * br - f_im * bi
    bbim_ref[...] = f_re * bi + f_im * br


def _ssm_prep(lam_re, lam_im, log_dt, b_re, b_im):
    rep = lambda a: jnp.repeat(a, SSM_GROUP, axis=0)
    lr = rep(lam_re)
    li = rep(lam_im)
    ldt = rep(jnp.broadcast_to(log_dt[:, None], (SSM_GROUPS, SSM_STATE)))
    br = b_re.transpose(0, 2, 1).reshape(SSM_WIDTH, SSM_STATE)
    bi = b_im.transpose(0, 2, 1).reshape(SSM_WIDTH, SSM_STATE)
    shp = jax.ShapeDtypeStruct((SSM_WIDTH, SSM_STATE), F32)
    return pl.pallas_call(
        _ssm_prep_kernel,
        out_shape=[shp] * 6,
        name="ssm_prep",
    )(lr, li, ldt, br, bi)


def _sublane_transpose(blocks):
    sub = lax.broadcasted_iota(jnp.int32, blocks[0].shape, 1)
    a = list(blocks)
    for s in (1, 2, 4):
        keep = (sub & s) == 0
        nxt = list(a)
        for r in range(SUBLANES):
            if r & s == 0:
                lo, hi = a[r], a[r + s]
                nxt[r] = jnp.where(keep, lo, pltpu.roll(hi, s, 1))
                nxt[r + s] = jnp.where(keep, pltpu.roll(lo, SUBLANES - s, 1), hi)
        a = nxt
    return a


def _ssm_kernel(u_ref, wb_ref, wc_ref, are_ref, aim_ref, alre_ref, alim_ref, d_ref,
                *refs, n_cast):
    cast_in = refs[:n_cast]
    z_ref = refs[n_cast]
    cast_out = refs[n_cast + 1:2 * n_cast + 1]
    up_ref, upb_ref, bu_ref, xb_ref, cst_ref, st_ref, zp_ref = refs[2 * n_cast + 1:]
    k = pl.program_id(2)
    L = SSM_CHUNK
    lh = L // SUBLANES

    for src, dst in zip(cast_in, cast_out):
        dst[...] = src[...].astype(BF16)

    @pl.when(k == 0)
    def _():
        cst_ref[...] = jnp.zeros_like(cst_ref)

    u4 = u_ref[...].astype(F32).reshape(SUBLANES, lh, SUBLANES, SSM_CH)
    for l_lo, blk in enumerate(_sublane_transpose([u4[r] for r in range(SUBLANES)])):
        up_ref[:, l_lo] = blk
    upb_ref[...] = up_ref[...].reshape(SSM_ROWS, SSM_CH).astype(BF16)

    for n in range(2 * SSM_ST // 256):
        bu_ref[:, n * 256:(n + 1) * 256] = jnp.dot(
            upb_ref[...], wb_ref[0, :, n * 256:(n + 1) * 256],
            preferred_element_type=F32)

    row_id = lax.broadcasted_iota(jnp.int32, (SUBLANES, SSM_COLS), 0)
    for cg in range(SSM_ST // SSM_COLS):
        re_sl = slice(cg * SSM_COLS, (cg + 1) * SSM_COLS)
        im_sl = slice(SSM_ST + cg * SSM_COLS, SSM_ST + (cg + 1) * SSM_COLS)
        ar = are_ref[0, :, re_sl]
        ai = aim_ref[0, :, re_sl]

        def step(l, sr, si):
            row = pl.multiple_of(l * SUBLANES, SUBLANES)
            br = bu_ref[pl.ds(row, SUBLANES), re_sl]
            bi = bu_ref[pl.ds(row, SUBLANES), im_sl]
            return ar * sr - ai * si + br, ar * si + ai * sr + bi

        def pass1(l, carry):
            return step(l, *carry)
        er, ei = lax.fori_loop(0, L, pass1, (cst_ref[:, re_sl], cst_ref[:, im_sl]),
                               unroll=2)

        alr = alre_ref[0, :, re_sl]
        ali = alim_ref[0, :, re_sl]
        tr, ti = er, ei
        for r in range(1, SUBLANES):
            pr = pltpu.roll(tr, 1, 0)
            pi = pltpu.roll(ti, 1, 0)
            tr = jnp.where(row_id == r, er + alr * pr - ali * pi, tr)
            ti = jnp.where(row_id == r, ei + alr * pi + ali * pr, ti)
        pr = pltpu.roll(tr, 1, 0)
        pi = pltpu.roll(ti, 1, 0)
        first = row_id == 0
        st_ref[:, re_sl] = jnp.where(first, cst_ref[:, re_sl], pr)
        st_ref[:, im_sl] = jnp.where(first, cst_ref[:, im_sl], pi)
        cst_ref[:, re_sl] = jnp.where(first, pr, 0.0)
        cst_ref[:, im_sl] = jnp.where(first, pi, 0.0)

        def pass2(m, carry):
            sr, si = carry
            r1, i1 = step(2 * m, sr, si)
            r2, i2 = step(2 * m + 1, r1, i1)
            row = pl.multiple_of(m * 2 * SUBLANES, 2 * SUBLANES)
            xb_ref[pl.ds(row, 2 * SUBLANES), re_sl] = (
                jnp.concatenate([r1, r2], axis=0).astype(BF16))
            xb_ref[pl.ds(row, 2 * SUBLANES), im_sl] = (
                jnp.concatenate([i1, i2], axis=0).astype(BF16))
            return r2, i2
        lax.fori_loop(0, L // 2, pass2, (st_ref[:, re_sl], st_ref[:, im_sl]))

    mrows = 256
    mh = mrows // (SUBLANES * SUBLANES)
    for m in range(SSM_ROWS // mrows):
        rs = slice(m * mrows, (m + 1) * mrows)
        y = jnp.dot(xb_ref[rs, :], wc_ref[0], preferred_element_type=F32)
        y = y + d_ref[0] * up_ref[m * mh:(m + 1) * mh].reshape(mrows, SSM_CH)
        zp_ref[m * mh:(m + 1) * mh] = jax.nn.gelu(y, approximate=True).reshape(
            mh, SUBLANES, SUBLANES, SSM_CH)

    blocks = _sublane_transpose([zp_ref[:, l_lo] for l_lo in range(SUBLANES)])
    for r, blk in enumerate(blocks):
        z_ref[r * L:(r + 1) * L, :] = blk.reshape(L, SSM_CH).astype(BF16)


def _ssm(proj, wb, wc, a_re, a_im, al_re, al_im, d, weights, seq):
    t = proj.shape[0]
    nb = t // seq
    nk = seq // SSM_ROWS
    steps = SSM_GB * nb * nk
    tok = lambda g, b, k: (b * nk + k, g)
    par = lambda g, b, k: (g, 0, 0)
    step = lambda g, b, k: ((g * nb + b) * nk + k, 0)
    lh = SSM_CHUNK // SUBLANES
    cast_specs = [pl.BlockSpec((w.shape[0] // steps, w.shape[1]), step) for w in weights]
    return pl.pallas_call(
        functools.partial(_ssm_kernel, n_cast=len(weights)),
        grid=(SSM_GB, nb, nk),
        in_specs=[pl.BlockSpec((SSM_ROWS, SSM_CH), tok),
                  pl.BlockSpec((1, SSM_CH, 2 * SSM_ST), par),
                  pl.BlockSpec((1, 2 * SSM_ST, SSM_CH), par),
                  pl.BlockSpec((1, SUBLANES, SSM_ST), par),
                  pl.BlockSpec((1, SUBLANES, SSM_ST), par),
                  pl.BlockSpec((1, SUBLANES, SSM_ST), par),
                  pl.BlockSpec((1, SUBLANES, SSM_ST), par),
                  pl.BlockSpec((1, 1, SSM_CH), par)] + cast_specs,
        out_specs=[pl.BlockSpec((SSM_ROWS, SSM_CH), tok)] + cast_specs,
        out_shape=[jax.ShapeDtypeStruct((t, SSM_WIDTH), BF16)]
        + [jax.ShapeDtypeStruct(w.shape, BF16) for w in weights],
        scratch_shapes=[pltpu.VMEM((lh, SUBLANES, SUBLANES, SSM_CH), F32),
                        pltpu.VMEM((SSM_ROWS, SSM_CH), BF16),
                        pltpu.VMEM((SSM_ROWS, 2 * SSM_ST), F32),
                        pltpu.VMEM((SSM_ROWS, 2 * SSM_ST), BF16),
                        pltpu.VMEM((SUBLANES, 2 * SSM_ST), F32),
                        pltpu.VMEM((SUBLANES, 2 * SSM_ST), F32),
                        pltpu.VMEM((lh, SUBLANES, SUBLANES, SSM_CH), F32)],
        compiler_params=_cparams(("arbitrary", "arbitrary", "arbitrary")),
        name="ssm",
    )(proj, wb, wc, a_re, a_im, al_re, al_im, d, *weights)


POOL_HALO = 16


def _pool_kernel(u_ref, w_ref, s_ref, o_ref, ext_ref, *, tt):
    k = pl.program_id(1)

    @pl.when(k == 0)
    def _():
        ext_ref[0:POOL_HALO, :] = jnp.zeros((POOL_HALO, POOL_WIDTH), F32)

    @pl.when(k != 0)
    def _():
        ext_ref[0:POOL_HALO, :] = ext_ref[tt:tt + POOL_HALO, :]

    ext_ref[POOL_HALO:POOL_HALO + tt, :] = u_ref[...].astype(F32)
    pos = (k * tt + lax.broadcasted_iota(jnp.int32, (tt, POOL_GROUP), 0)).astype(F32)
    for gi, w in enumerate(POOL_WINDOWS):
        cs = slice(gi * POOL_GROUP, (gi + 1) * POOL_GROUP)
        cur = ext_ref[POOL_HALO:POOL_HALO + tt, cs]
        tot = cur
        for j in range(1, w):
            tot = tot + ext_ref[POOL_HALO - j:POOL_HALO - j + tt, cs]
        count = jnp.minimum(pos + 1.0, float(w))
        pooled = tot / count - cur
        y = jnp.dot(pooled.astype(BF16), w_ref[gi], preferred_element_type=F32)
        o_ref[:, cs] = (y * s_ref[:, cs]).astype(BF16)


def _pool(proj, w_pool, pool_scale, seq):
    tt = 512
    t = proj.shape[0]
    nk = seq // tt
    return pl.pallas_call(
        functools.partial(_pool_kernel, tt=tt),
        grid=(t // seq, nk),
        in_specs=[pl.BlockSpec((tt, POOL_WIDTH), lambda b, k: (b * nk + k, 1)),
                  pl.BlockSpec((len(POOL_WINDOWS), POOL_GROUP, POOL_GROUP),
                               lambda b, k: (0, 0, 0)),
                  pl.BlockSpec((1, POOL_WIDTH), lambda b, k: (0, 0))],
        out_specs=pl.BlockSpec((tt, POOL_WIDTH), lambda b, k: (b * nk + k, 0)),
        out_shape=jax.ShapeDtypeStruct((t, POOL_WIDTH), BF16),
        scratch_shapes=[pltpu.VMEM((tt + POOL_HALO, POOL_WIDTH), F32)],
        compiler_params=_cparams(("parallel", "arbitrary")),
        name="pool",
    )(proj, w_pool, pool_scale)


def _merge_kernel(z_ref, yp_ref, ga_ref, gb_ref, wv_ref, wg_ref, wp_ref, o_ref):
    half = z_ref.shape[0] // 2
    for s in range(2):
        rs = slice(s * half, (s + 1) * half)
        z = z_ref[rs, :]
        val = jnp.dot(z, wv_ref[...], preferred_element_type=F32)
        gate = jnp.dot(z, wg_ref[...], preferred_element_type=F32)
        y_a = val * jax.nn.sigmoid(gate)
        y_b = jnp.dot(yp_ref[rs, :], wp_ref[...], preferred_element_type=F32)
        merged = (jax.nn.sigmoid(ga_ref[rs, :].astype(F32)) * y_a
                  + jax.nn.sigmoid(gb_ref[rs, :].astype(F32)) * y_b)
        o_ref[rs, :] = merged.astype(BF16)


def _merge(z, yp, proj, w_val, w_gate, w_po):
    tm, tn = 1024, 512
    t = z.shape[0]
    ga0 = (SSM_WIDTH + POOL_WIDTH) // tn
    gb0 = ga0 + D_MODEL // tn
    wspec = pl.BlockSpec((SSM_WIDTH, tn), lambda i, j: (0, j))
    return pl.pallas_call(
        _merge_kernel,
        grid=(t // tm, D_MODEL // tn),
        in_specs=[pl.BlockSpec((tm, SSM_WIDTH), lambda i, j: (i, 0)),
                  pl.BlockSpec((tm, POOL_WIDTH), lambda i, j: (i, 0)),
                  pl.BlockSpec((tm, tn), lambda i, j: (i, ga0 + j)),
                  pl.BlockSpec((tm, tn), lambda i, j: (i, gb0 + j)),
                  wspec, wspec, wspec],
        out_specs=pl.BlockSpec((tm, tn), lambda i, j: (i, j)),
        out_shape=jax.ShapeDtypeStruct((t, D_MODEL), BF16),
        compiler_params=_cparams(("parallel", "arbitrary")),
        name="merge",
    )(z, yp, proj, proj, w_val, w_gate, w_po)


def _out_ln_kernel(m_ref, w_ref, x_ref, g1_ref, lg_ref, lb_ref, sh_ref, sc_ref,
                   x1_ref, h2_ref):
    half = m_ref.shape[0] // 2
    for s in range(2):
        rs = slice(s * half, (s + 1) * half)
        y = jnp.dot(m_ref[rs, :], w_ref[...], preferred_element_type=F32)
        r = ALPHA * x_ref[rs, :] + g1_ref[0] * y
        x1 = _ln(r) * lg_ref[...] + lb_ref[...]
        x1_ref[rs, :] = x1
        h2_ref[rs, :] = (_ln(x1) * (1.0 + sc_ref[0]) + sh_ref[0]).astype(BF16)


def _out_ln(merged, w_out, x2, g1, ln_g, ln_b, sh2, sc2, seq):
    tm = 512
    t = x2.shape[0]
    per_batch = seq // tm
    row = pl.BlockSpec((tm, D_MODEL), lambda i: (i, 0))
    vec = pl.BlockSpec((1, D_MODEL), lambda i: (0, 0))
    bvec = pl.BlockSpec((1, 1, D_MODEL), lambda i: (i // per_batch, 0, 0))
    return pl.pallas_call(
        _out_ln_kernel,
        grid=(t // tm,),
        in_specs=[row,
                  pl.BlockSpec((D_MODEL, D_MODEL), lambda i: (0, 0),
                               pipeline_mode=pl.Buffered(1)),
                  row, bvec, vec, vec, bvec, bvec],
        out_specs=[row, row],
        out_shape=[jax.ShapeDtypeStruct((t, D_MODEL), F32),
                   jax.ShapeDtypeStruct((t, D_MODEL), BF16)],
        compiler_params=_cparams(("parallel",)),
        name="out_ln",
    )(merged, w_out, x2, g1, ln_g, ln_b, sh2, sc2)


def _mlp_kernel(h_ref, w1_ref, w2_ref, x1_ref, g2_ref, lg_ref, lb_ref, o_ref):
    f = pl.program_id(1)

    @pl.when(f == 0)
    def _():
        o_ref[...] = jnp.zeros_like(o_ref)

    a = jnp.maximum(jnp.dot(h_ref[...], w1_ref[...], preferred_element_type=F32), 0.0)
    o_ref[...] += jnp.dot((a * a).astype(BF16), w2_ref[...], preferred_element_type=F32)

    @pl.when(f == pl.num_programs(1) - 1)
    def _():
        r = ALPHA * x1_ref[...] + g2_ref[0] * o_ref[...]
        o_ref[...] = _ln(r) * lg_ref[...] + lb_ref[...]


def _mlp(h2, w1, w2, x1, g2, ln_g, ln_b, seq):
    tm, tf = 512, 1024
    t = h2.shape[0]
    per_batch = seq // tm
    once = pl.Buffered(1)
    vec = pl.BlockSpec((1, D_MODEL), lambda i, f: (0, 0))
    return pl.pallas_call(
        _mlp_kernel,
        grid=(t // tm, D_FF // tf),
        in_specs=[pl.BlockSpec((tm, D_MODEL), lambda i, f: (i, 0), pipeline_mode=once),
                  pl.BlockSpec((D_MODEL, tf), lambda i, f: (0, f)),
                  pl.BlockSpec((tf, D_MODEL), lambda i, f: (f, 0)),
                  pl.BlockSpec((tm, D_MODEL), lambda i, f: (i, 0), pipeline_mode=once),
                  pl.BlockSpec((1, 1, D_MODEL), lambda i, f: (i // per_batch, 0, 0)),
                  vec, vec],
        out_specs=pl.BlockSpec((tm, D_MODEL), lambda i, f: (i, 0)),
        out_shape=jax.ShapeDtypeStruct((t, D_MODEL), F32),
        compiler_params=_cparams(("parallel", "arbitrary")),
        name="mlp",
    )(h2, w1, w2, x1, g2, ln_g, ln_b)


def _block_diag_weights(bb_re, bb_im, c_re, c_im):
    eye = jnp.eye(SSM_CH // SSM_GROUP, dtype=F32)
    ng = SSM_CH // SSM_GROUP

    def b_blocks(bb):
        b4 = bb.reshape(SSM_GB, ng, SSM_GROUP, SSM_STATE)
        return jnp.einsum('bghp,gk->bghkp', b4, eye).reshape(SSM_GB, SSM_CH, SSM_ST)

    def c_blocks(c):
        c4 = c.reshape(SSM_GB, ng, SSM_GROUP, SSM_STATE)
        return jnp.einsum('bghp,gk->bkpgh', c4, eye).reshape(SSM_GB, SSM_ST, SSM_CH)

    wb = jnp.concatenate([b_blocks(bb_re), b_blocks(bb_im)], axis=-1).astype(BF16)
    wc = jnp.concatenate([c_blocks(c_re), -c_blocks(c_im)], axis=1).astype(BF16)
    return wb, wc


def kernel(x, c, w_ada, b_ada, w_in, lam_re, lam_im, log_dt, ssm_b_re, ssm_b_im, ssm_c_re, ssm_c_im, ssm_d, w_glu_val, w_glu_gate, w_pool, pool_scale, w_pool_out, w_out, ln1_g, ln1_b, w_ff1, w_ff2, ln2_g, ln2_b):
    bsz, seq, d = x.shape
    t = bsz * seq
    l = 0
    x2 = x.reshape(t, d)

    c_pad = jnp.zeros((SUBLANES, d), F32).at[:bsz].set(c)
    mod = _mod(c_pad, w_ada[l], b_ada[l][None, :])[:bsz]
    sh1, sc1, g1, sh2, sc2, g2 = [m[:, None, :] for m in jnp.split(mod, N_MOD, axis=-1)]

    proj = _in_proj(x2, sh1, sc1, w_in[l].astype(BF16), seq)

    a_re, a_im, al_re, al_im, bb_re, bb_im = _ssm_prep(
        lam_re[l], lam_im[l], log_dt[l], ssm_b_re[l], ssm_b_im[l])
    wb, wc = _block_diag_weights(bb_re, bb_im, ssm_c_re[l], ssm_c_im[l])

    def state_rows(a):
        a = a[::SSM_GROUP].reshape(SSM_GB, 1, SSM_ST)
        return jnp.broadcast_to(a, (SSM_GB, SUBLANES, SSM_ST))

    later = [w_ff1[l], w_ff2[l], w_out[l], w_glu_val[l], w_glu_gate[l], w_pool_out[l]]
    z, w1_bf, w2_bf, wo_bf, wv_bf, wg_bf, wp_bf = _ssm(
        proj, wb, wc, state_rows(a_re), state_rows(a_im), state_rows(al_re),
        state_rows(al_im), ssm_d[l].reshape(SSM_GB, 1, SSM_CH), later, seq)

    yp = _pool(proj, w_pool[l].astype(BF16), pool_scale[l][None, :], seq)

    merged = _merge(z, yp, proj, wv_bf, wg_bf, wp_bf)

    x1, h2 = _out_ln(merged, wo_bf, x2, g1, ln1_g[l][None, :],
                     ln1_b[l][None, :], sh2, sc2, seq)

    out = _mlp(h2, w1_bf, w2_bf, x1, g2,
               ln2_g[l][None, :], ln2_b[l][None, :], seq)
    return out.reshape(bsz, seq, d)
```

```python
import functools
import math

import jax
import jax.numpy as jnp
from jax import lax
from jax.experimental import pallas as pl
from jax.experimental.pallas import tpu as pltpu

D_MODEL = 2048
SSM_WIDTH = 1024
SSM_GROUP = 16
SSM_GROUPS = 64
SSM_STATE = 64
POOL_WIDTH = 1024
POOL_WINDOWS = (2, 4, 8, 16)
POOL_GROUP = 256
IN_WIDTH = SSM_WIDTH + POOL_WIDTH + 2 * D_MODEL
D_FF = 4 * D_MODEL
N_MOD = 6
LN_EPS = 1e-5
ALPHA = 2.0 ** 0.25

SUBLANES = 8
LANES = 128
VMEM_LIMIT = 56 * 1024 * 1024

SSM_CHUNK = 128
SSM_ROWS = SUBLANES * SSM_CHUNK
SSM_GB = 4
SSM_CH = SSM_WIDTH // SSM_GB
SSM_ST = SSM_CH // SSM_GROUP * SSM_STATE
SSM_COLS = 512

BF16 = jnp.bfloat16
F32 = jnp.float32


def _cparams(sem):
    return pltpu.CompilerParams(dimension_semantics=sem, vmem_limit_bytes=VMEM_LIMIT)


def _ln(x):
    mu = jnp.mean(x, axis=-1, keepdims=True)
    xc = x - mu
    var = jnp.mean(xc * xc, axis=-1, keepdims=True)
    return xc * lax.rsqrt(var + LN_EPS)


def _mod_kernel(c_ref, w_ref, b_ref, o_ref):
    c = c_ref[...]
    c_act = (c * jax.nn.sigmoid(c)).astype(BF16)
    o_ref[...] = jnp.dot(c_act, w_ref[...].astype(BF16),
                         preferred_element_type=F32) + b_ref[...]


def _mod(c_pad, w_ada, b_ada):
    tn = 1024
    n = w_ada.shape[1]
    return pl.pallas_call(
        _mod_kernel,
        grid=(n // tn,),
        in_specs=[pl.BlockSpec((SUBLANES, D_MODEL), lambda j: (0, 0)),
                  pl.BlockSpec((D_MODEL, tn), lambda j: (0, j)),
                  pl.BlockSpec((1, tn), lambda j: (0, j))],
        out_specs=pl.BlockSpec((SUBLANES, tn), lambda j: (0, j)),
        out_shape=jax.ShapeDtypeStruct((SUBLANES, n), F32),
        compiler_params=_cparams(("arbitrary",)),
        name="mod",
    )(c_pad, w_ada, b_ada)


def _in_proj_kernel(x_ref, sh_ref, sc_ref, w_ref, proj_ref, h_ref):
    j = pl.program_id(1)
    half = x_ref.shape[0] // 2

    @pl.when(j == 0)
    def _():
        for s in range(2):
            rs = slice(s * half, (s + 1) * half)
            y = _ln(x_ref[rs, :])
            h_ref[rs, :] = (y * (1.0 + sc_ref[0]) + sh_ref[0]).astype(BF16)
            proj_ref[rs, :] = jnp.dot(h_ref[rs, :], w_ref[...],
                                      preferred_element_type=F32).astype(BF16)

    @pl.when(j != 0)
    def _():
        proj_ref[...] = jnp.dot(h_ref[...], w_ref[...],
                                preferred_element_type=F32).astype(BF16)


def _in_proj(x2, sh, sc, w_in, seq):
    tm, tn = 1024, 1024
    t = x2.shape[0]
    per_batch = seq // tm
    return pl.pallas_call(
        _in_proj_kernel,
        grid=(t // tm, IN_WIDTH // tn),
        in_specs=[pl.BlockSpec((tm, D_MODEL), lambda i, j: (i, 0)),
                  pl.BlockSpec((1, 1, D_MODEL), lambda i, j: (i // per_batch, 0, 0)),
                  pl.BlockSpec((1, 1, D_MODEL), lambda i, j: (i // per_batch, 0, 0)),
                  pl.BlockSpec((D_MODEL, tn), lambda i, j: (0, j))],
        out_specs=pl.BlockSpec((tm, tn), lambda i, j: (i, j)),
        out_shape=jax.ShapeDtypeStruct((t, IN_WIDTH), BF16),
        scratch_shapes=[pltpu.VMEM((tm, D_MODEL), BF16)],
        compiler_params=_cparams(("parallel", "arbitrary")),
        name="in_proj",
    )(x2, sh, sc, w_in)


def _ssm_prep_kernel(lr_ref, li_ref, ldt_ref, br_ref, bi_ref,
                     are_ref, aim_ref, alre_ref, alim_ref, bbre_ref, bbim_ref):
    lr = lr_ref[...]
    li = li_ref[...]
    dt = jnp.exp(ldt_ref[...])
    mag = jnp.exp(lr * dt)
    ang = li * dt
    ab_re = mag * jnp.cos(ang)
    ab_im = mag * jnp.sin(ang)
    are_ref[...] = ab_re
    aim_ref[...] = ab_im
    mag_l = jnp.exp(lr * dt * float(SSM_CHUNK))
    ang_l = ang * float(SSM_CHUNK)
    alre_ref[...] = mag_l * jnp.cos(ang_l)
    alim_ref[...] = mag_l * jnp.sin(ang_l)
    num_re = ab_re - 1.0
    num_im = ab_im
    den = lr * lr + li * li
    f_re = (num_re * lr + num_im * li) / den
    f_im = (num_im * lr - num_re * li) / den
    br = br_ref[...]
    bi = bi_ref[...]
    bbre_ref[...] = f_re * br - f_im * bi
    bbim_ref[...] = f_re * bi + f_im * br


def _ssm_prep(lam_re, lam_im, log_dt, b_re, b_im):
    rep = lambda a: jnp.repeat(a, SSM_GROUP, axis=0)
    lr = rep(lam_re)
    li = rep(lam_im)
    ldt = rep(jnp.broadcast_to(log_dt[:, None], (SSM_GROUPS, SSM_STATE)))
    br = b_re.transpose(0, 2, 1).reshape(SSM_WIDTH, SSM_STATE)
    bi = b_im.transpose(0, 2, 1).reshape(SSM_WIDTH, SSM_STATE)
    shp = jax.ShapeDtypeStruct((SSM_WIDTH, SSM_STATE), F32)
    return pl.pallas_call(
        _ssm_prep_kernel,
        out_shape=[shp] * 6,
        name="ssm_prep",
    )(lr, li, ldt, br, bi)


def _sublane_transpose(blocks):
    sub = lax.broadcasted_iota(jnp.int32, blocks[0].shape, 1)
    a = list(blocks)
    for s in (1, 2, 4):
        keep = (sub & s) == 0
        nxt = list(a)
        for r in range(SUBLANES):
            if r & s == 0:
                lo, hi = a[r], a[r + s]
                nxt[r] = jnp.where(keep, lo, pltpu.roll(hi, s, 1))
                nxt[r + s] = jnp.where(keep, pltpu.roll(lo, SUBLANES - s, 1), hi)
        a = nxt
    return a


def _ssm_kernel(u_ref, wb_ref, wct_ref, a_ref, d_ref, *refs, n_cast):
    cast_in = refs[:n_cast]
    z_ref = refs[n_cast]
    cast_out = refs[n_cast + 1:2 * n_cast + 1]
    up_ref, upb_ref, bu_ref, xb_ref, cst_ref, st_ref, zp_ref = refs[2 * n_cast + 1:]
    k = pl.program_id(2)
    L = SSM_CHUNK
    lh = L // SUBLANES

    for src, dst in zip(cast_in, cast_out):
        dst[...] = src[...].astype(BF16)

    @pl.when(k == 0)
    def _():
        cst_ref[...] = jnp.zeros_like(cst_ref)

    u4 = u_ref[...].astype(F32).reshape(SUBLANES, lh, SUBLANES, SSM_CH)
    for l_lo, blk in enumerate(_sublane_transpose([u4[r] for r in range(SUBLANES)])):
        up_ref[:, l_lo] = blk
    upb_ref[...] = up_ref[...].reshape(SSM_ROWS, SSM_CH).astype(BF16)

    for n in range(2 * SSM_ST // 256):
        bu_ref[:, n * 256:(n + 1) * 256] = jnp.dot(
            upb_ref[...], wb_ref[0, :, n * 256:(n + 1) * 256],
            preferred_element_type=F32)

    row_id = lax.broadcasted_iota(jnp.int32, (SUBLANES, SSM_COLS), 0)
    for cg in range(SSM_ST // SSM_COLS):
        re_sl = slice(cg * SSM_COLS, (cg + 1) * SSM_COLS)
        im_sl = slice(SSM_ST + cg * SSM_COLS, SSM_ST + (cg + 1) * SSM_COLS)
        ar = a_ref[0, 0, :, re_sl]
        ai = a_ref[1, 0, :, re_sl]

        def step(l, sr, si):
            row = pl.multiple_of(l * SUBLANES, SUBLANES)
            br = bu_ref[pl.ds(row, SUBLANES), re_sl]
            bi = bu_ref[pl.ds(row, SUBLANES), im_sl]
            return ar * sr - ai * si + br, ar * si + ai * sr + bi

        def pass1(l, carry):
            return step(l, *carry)
        er, ei = lax.fori_loop(0, L, pass1, (cst_ref[:, re_sl], cst_ref[:, im_sl]),
                               unroll=2)

        alr = a_ref[2, 0, :, re_sl]
        ali = a_ref[3, 0, :, re_sl]
        tr, ti = er, ei
        for r in range(1, SUBLANES):
            pr = pltpu.roll(tr, 1, 0)
            pi = pltpu.roll(ti, 1, 0)
            tr = jnp.where(row_id == r, er + alr * pr - ali * pi, tr)
            ti = jnp.where(row_id == r, ei + alr * pi + ali * pr, ti)
        pr = pltpu.roll(tr, 1, 0)
        pi = pltpu.roll(ti, 1, 0)
        first = row_id == 0
        st_ref[:, re_sl] = jnp.where(first, cst_ref[:, re_sl], pr)
        st_ref[:, im_sl] = jnp.where(first, cst_ref[:, im_sl], pi)
        cst_ref[:, re_sl] = jnp.where(first, pr, 0.0)
        cst_ref[:, im_sl] = jnp.where(first, pi, 0.0)

        def pass2(m, carry):
            sr, si = carry
            r1, i1 = step(2 * m, sr, si)
            r2, i2 = step(2 * m + 1, r1, i1)
            row = pl.multiple_of(m * 2 * SUBLANES, 2 * SUBLANES)
            xb_ref[pl.ds(row, 2 * SUBLANES), re_sl] = (
                jnp.concatenate([r1, r2], axis=0).astype(BF16))
            xb_ref[pl.ds(row, 2 * SUBLANES), im_sl] = (
                jnp.concatenate([i1, i2], axis=0).astype(BF16))
            return r2, i2
        lax.fori_loop(0, L // 2, pass2, (st_ref[:, re_sl], st_ref[:, im_sl]))

    mrows = 256
    mh = mrows // (SUBLANES * SUBLANES)
    for m in range(SSM_ROWS // mrows):
        rs = slice(m * mrows, (m + 1) * mrows)
        y = lax.dot_general(xb_ref[rs, :], wct_ref[0], (((1,), (1,)), ((), ())),
                            preferred_element_type=F32)
        y = y + d_ref[0] * up_ref[m * mh:(m + 1) * mh].reshape(mrows, SSM_CH)
        zp_ref[m * mh:(m + 1) * mh] = jax.nn.gelu(y, approximate=True).reshape(
            mh, SUBLANES, SUBLANES, SSM_CH)

    blocks = _sublane_transpose([zp_ref[:, l_lo] for l_lo in range(SUBLANES)])
    for r, blk in enumerate(blocks):
        z_ref[r * L:(r + 1) * L, :] = blk.reshape(L, SSM_CH).astype(BF16)


def _ssm(proj, wb, wct, a_all, d, weights, seq):
    t = proj.shape[0]
    nb = t // seq
    nk = seq // SSM_ROWS
    steps = SSM_GB * nb * nk
    tok = lambda g, b, k: (b * nk + k, g)
    par = lambda g, b, k: (g, 0, 0)
    step = lambda g, b, k: ((g * nb + b) * nk + k, 0)
    lh = SSM_CHUNK // SUBLANES
    cast_specs = [pl.BlockSpec((w.shape[0] // steps, w.shape[1]), step) for w in weights]
    return pl.pallas_call(
        functools.partial(_ssm_kernel, n_cast=len(weights)),
        grid=(SSM_GB, nb, nk),
        in_specs=[pl.BlockSpec((SSM_ROWS, SSM_CH), tok),
                  pl.BlockSpec((1, SSM_CH, 2 * SSM_ST), par),
                  pl.BlockSpec((1, SSM_CH, 2 * SSM_ST), par),
                  pl.BlockSpec((4, 1, SUBLANES, SSM_ST), lambda g, b, k: (0, g, 0, 0)),
                  pl.BlockSpec((1, 1, SSM_CH), par)] + cast_specs,
        out_specs=[pl.BlockSpec((SSM_ROWS, SSM_CH), tok)] + cast_specs,
        out_shape=[jax.ShapeDtypeStruct((t, SSM_WIDTH), BF16)]
        + [jax.ShapeDtypeStruct(w.shape, BF16) for w in weights],
        scratch_shapes=[pltpu.VMEM((lh, SUBLANES, SUBLANES, SSM_CH), F32),
                        pltpu.VMEM((SSM_ROWS, SSM_CH), BF16),
                        pltpu.VMEM((SSM_ROWS, 2 * SSM_ST), F32),
                        pltpu.VMEM((SSM_ROWS, 2 * SSM_ST), BF16),
                        pltpu.VMEM((SUBLANES, 2 * SSM_ST), F32),
                        pltpu.VMEM((SUBLANES, 2 * SSM_ST), F32),
                        pltpu.VMEM((lh, SUBLANES, SUBLANES, SSM_CH), F32)],
        compiler_params=_cparams(("arbitrary", "arbitrary", "arbitrary")),
        name="ssm",
    )(proj, wb, wct, a_all, d, *weights)


POOL_HALO = 16


def _pool_kernel(u_ref, w_ref, s_ref, o_ref, ext_ref, *, tt):
    k = pl.program_id(1)

    @pl.when(k == 0)
    def _():
        ext_ref[0:POOL_HALO, :] = jnp.zeros((POOL_HALO, POOL_WIDTH), F32)

    @pl.when(k != 0)
    def _():
        ext_ref[0:POOL_HALO, :] = ext_ref[tt:tt + POOL_HALO, :]

    ext_ref[POOL_HALO:POOL_HALO + tt, :] = u_ref[...].astype(F32)
    pos = (k * tt + lax.broadcasted_iota(jnp.int32, (tt, POOL_GROUP), 0)).astype(F32)
    for gi, w in enumerate(POOL_WINDOWS):
        cs = slice(gi * POOL_GROUP, (gi + 1) * POOL_GROUP)
        cur = ext_ref[POOL_HALO:POOL_HALO + tt, cs]
        tot = cur
        for j in range(1, w):
            tot = tot + ext_ref[POOL_HALO - j:POOL_HALO - j + tt, cs]
        count = jnp.minimum(pos + 1.0, float(w))
        pooled = tot / count - cur
        y = jnp.dot(pooled.astype(BF16), w_ref[gi], preferred_element_type=F32)
        o_ref[:, cs] = (y * s_ref[:, cs]).astype(BF16)


def _pool(proj, w_pool, pool_scale, seq):
    tt = 512
    t = proj.shape[0]
    nk = seq // tt
    return pl.pallas_call(
        functools.partial(_pool_kernel, tt=tt),
        grid=(t // seq, nk),
        in_specs=[pl.BlockSpec((tt, POOL_WIDTH), lambda b, k: (b * nk + k, 1)),
                  pl.BlockSpec((len(POOL_WINDOWS), POOL_GROUP, POOL_GROUP),
                               lambda b, k: (0, 0, 0)),
                  pl.BlockSpec((1, POOL_WIDTH), lambda b, k: (0, 0))],
        out_specs=pl.BlockSpec((tt, POOL_WIDTH), lambda b, k: (b * nk + k, 0)),
        out_shape=jax.ShapeDtypeStruct((t, POOL_WIDTH), BF16),
        scratch_shapes=[pltpu.VMEM((tt + POOL_HALO, POOL_WIDTH), F32)],
        compiler_params=_cparams(("parallel", "arbitrary")),
        name="pool",
    )(proj, w_pool, pool_scale)


def _merge_kernel(z_ref, yp_ref, ga_ref, gb_ref, wv_ref, wg_ref, wp_ref, o_ref):
    half = z_ref.shape[0] // 2
    for s in range(2):
        rs = slice(s * half, (s + 1) * half)
        z = z_ref[rs, :]
        val = jnp.dot(z, wv_ref[...], preferred_element_type=F32)
        gate = jnp.dot(z, wg_ref[...], preferred_element_type=F32)
        y_a = val * jax.nn.sigmoid(gate)
        y_b = jnp.dot(yp_ref[rs, :], wp_ref[...], preferred_element_type=F32)
        merged = (jax.nn.sigmoid(ga_ref[rs, :].astype(F32)) * y_a
                  + jax.nn.sigmoid(gb_ref[rs, :].astype(F32)) * y_b)
        o_ref[rs, :] = merged.astype(BF16)


def _merge(z, yp, proj, w_val, w_gate, w_po):
    tm, tn = 1024, 512
    t = z.shape[0]
    ga0 = (SSM_WIDTH + POOL_WIDTH) // tn
    gb0 = ga0 + D_MODEL // tn
    wspec = pl.BlockSpec((SSM_WIDTH, tn), lambda i, j: (0, j))
    return pl.pallas_call(
        _merge_kernel,
        grid=(t // tm, D_MODEL // tn),
        in_specs=[pl.BlockSpec((tm, SSM_WIDTH), lambda i, j: (i, 0)),
                  pl.BlockSpec((tm, POOL_WIDTH), lambda i, j: (i, 0)),
                  pl.BlockSpec((tm, tn), lambda i, j: (i, ga0 + j)),
                  pl.BlockSpec((tm, tn), lambda i, j: (i, gb0 + j)),
                  wspec, wspec, wspec],
        out_specs=pl.BlockSpec((tm, tn), lambda i, j: (i, j)),
        out_shape=jax.ShapeDtypeStruct((t, D_MODEL), BF16),
        compiler_params=_cparams(("parallel", "arbitrary")),
        name="merge",
    )(z, yp, proj, proj, w_val, w_gate, w_po)


def _out_ln_kernel(m_ref, w_ref, x_ref, g1_ref, lg_ref, lb_ref, sh_ref, sc_ref,
                   x1_ref, h2_ref):
    half = m_ref.shape[0] // 2
    for s in range(2):
        rs = slice(s * half, (s + 1) * half)
        y = jnp.dot(m_ref[rs, :], w_ref[...], preferred_element_type=F32)
        r = ALPHA * x_ref[rs, :] + g1_ref[0] * y
        x1 = _ln(r) * lg_ref[...] + lb_ref[...]
        x1_ref[rs, :] = x1
        h2_ref[rs, :] = (_ln(x1) * (1.0 + sc_ref[0]) + sh_ref[0]).astype(BF16)


def _out_ln(merged, w_out, x2, g1, ln_g, ln_b, sh2, sc2, seq):
    tm = 512
    t = x2.shape[0]
    per_batch = seq // tm
    row = pl.BlockSpec((tm, D_MODEL), lambda i: (i, 0))
    vec = pl.BlockSpec((1, D_MODEL), lambda i: (0, 0))
    bvec = pl.BlockSpec((1, 1, D_MODEL), lambda i: (i // per_batch, 0, 0))
    return pl.pallas_call(
        _out_ln_kernel,
        grid=(t // tm,),
        in_specs=[row,
                  pl.BlockSpec((D_MODEL, D_MODEL), lambda i: (0, 0),
                               pipeline_mode=pl.Buffered(1)),
                  row, bvec, vec, vec, bvec, bvec],
        out_specs=[row, row],
        out_shape=[jax.ShapeDtypeStruct((t, D_MODEL), F32),
                   jax.ShapeDtypeStruct((t, D_MODEL), BF16)],
        compiler_params=_cparams(("parallel",)),
        name="out_ln",
    )(merged, w_out, x2, g1, ln_g, ln_b, sh2, sc2)


def _mlp_kernel(h_ref, w1_ref, w2_ref, x1_ref, g2_ref, lg_ref, lb_ref, o_ref):
    f = pl.program_id(1)
    last = pl.num_programs(1) - 1
    half = h_ref.shape[0] // 2

    def ff(rows):
        a = jnp.maximum(jnp.dot(h_ref[rows, :], w1_ref[...],
                                preferred_element_type=F32), 0.0)
        return jnp.dot((a * a).astype(BF16), w2_ref[...], preferred_element_type=F32)

    @pl.when(f == 0)
    def _():
        o_ref[...] = ff(slice(None))

    @pl.when(jnp.logical_and(f > 0, f < last))
    def _():
        o_ref[...] += ff(slice(None))

    @pl.when(f == last)
    def _():
        for s in range(2):
            rs = slice(s * half, (s + 1) * half)
            r = ALPHA * x1_ref[rs, :] + g2_ref[0] * (o_ref[rs, :] + ff(rs))
            o_ref[rs, :] = _ln(r) * lg_ref[...] + lb_ref[...]


def _mlp(h2, w1, w2, x1, g2, ln_g, ln_b, seq):
    tm, tf = 512, 1024
    t = h2.shape[0]
    per_batch = seq // tm
    once = pl.Buffered(1)
    vec = pl.BlockSpec((1, D_MODEL), lambda i, f: (0, 0))
    return pl.pallas_call(
        _mlp_kernel,
        grid=(t // tm, D_FF // tf),
        in_specs=[pl.BlockSpec((tm, D_MODEL), lambda i, f: (i, 0), pipeline_mode=once),
                  pl.BlockSpec((D_MODEL, tf), lambda i, f: (0, f)),
                  pl.BlockSpec((tf, D_MODEL), lambda i, f: (f, 0)),
                  pl.BlockSpec((tm, D_MODEL), lambda i, f: (i, 0), pipeline_mode=once),
                  pl.BlockSpec((1, 1, D_MODEL), lambda i, f: (i // per_batch, 0, 0)),
                  vec, vec],
        out_specs=pl.BlockSpec((tm, D_MODEL), lambda i, f: (i, 0)),
        out_shape=jax.ShapeDtypeStruct((t, D_MODEL), F32),
        compiler_params=_cparams(("parallel", "arbitrary")),
        name="mlp",
    )(h2, w1, w2, x1, g2, ln_g, ln_b)


def _block_diag_weights(bb_re, bb_im, c_re, c_im):
    ng = SSM_CH // SSM_GROUP
    rows = jnp.arange(SSM_CH)[:, None] // SSM_GROUP
    cols = jnp.arange(SSM_ST)[None, :] // SSM_STATE
    on_diag = rows == cols

    def blocks(w):
        w = w.reshape(SSM_GB, SSM_CH, SSM_STATE)
        return jnp.where(on_diag, jnp.tile(w, (1, 1, ng)), 0.0)

    c_re = c_re.reshape(SSM_WIDTH, SSM_STATE)
    c_im = c_im.reshape(SSM_WIDTH, SSM_STATE)
    wb = jnp.concatenate([blocks(bb_re), blocks(bb_im)], axis=-1).astype(BF16)
    wct = jnp.concatenate([blocks(c_re), blocks(-c_im)], axis=-1).astype(BF16)
    return wb, wct


def kernel(x, c, w_ada, b_ada, w_in, lam_re, lam_im, log_dt, ssm_b_re, ssm_b_im, ssm_c_re, ssm_c_im, ssm_d, w_glu_val, w_glu_gate, w_pool, pool_scale, w_pool_out, w_out, ln1_g, ln1_b, w_ff1, w_ff2, ln2_g, ln2_b):
    bsz, seq, d = x.shape
    t = bsz * seq
    l = 0
    x2 = x.reshape(t, d)

    c_pad = jnp.zeros((SUBLANES, d), F32).at[:bsz].set(c)
    mod = _mod(c_pad, w_ada[l], b_ada[l][None, :])[:bsz]
    sh1, sc1, g1, sh2, sc2, g2 = [m[:, None, :] for m in jnp.split(mod, N_MOD, axis=-1)]

    proj = _in_proj(x2, sh1, sc1, w_in[l].astype(BF16), seq)

    a_re, a_im, al_re, al_im, bb_re, bb_im = _ssm_prep(
        lam_re[l], lam_im[l], log_dt[l], ssm_b_re[l], ssm_b_im[l])
    wb, wct = _block_diag_weights(bb_re, bb_im, ssm_c_re[l], ssm_c_im[l])
    a_all = jnp.stack([a_re, a_im, al_re, al_im])[:, ::SSM_GROUP]
    a_all = jnp.broadcast_to(a_all.reshape(4, SSM_GB, 1, SSM_ST),
                             (4, SSM_GB, SUBLANES, SSM_ST))

    later = [w_ff1[l], w_ff2[l], w_out[l], w_glu_val[l], w_glu_gate[l], w_pool_out[l]]
    z, w1_bf, w2_bf, wo_bf, wv_bf, wg_bf, wp_bf = _ssm(
        proj, wb, wct, a_all, ssm_d[l].reshape(SSM_GB, 1, SSM_CH), later, seq)

    yp = _pool(proj, w_pool[l].astype(BF16), pool_scale[l][None, :], seq)

    merged = _merge(z, yp, proj, wv_bf, wg_bf, wp_bf)

    x1, h2 = _out_ln(merged, wo_bf, x2, g1, ln1_g[l][None, :],
                     ln1_b[l][None, :], sh2, sc2, seq)

    out = _mlp(h2, w1_bf, w2_bf, x1, g2,
               ln2_g[l][None, :], ln2_b[l][None, :], seq)
    return out.reshape(bsz, seq, d)
```

```python
import functools
import math

import jax
import jax.numpy as jnp
from jax import lax
from jax.experimental import pallas as pl
from jax.experimental.pallas import tpu as pltpu

D_MODEL = 2048
SSM_WIDTH = 1024
SSM_GROUP = 16
SSM_GROUPS = 64
SSM_STATE = 64
POOL_WIDTH = 1024
POOL_WINDOWS = (2, 4, 8, 16)
POOL_GROUP = 256
IN_WIDTH = SSM_WIDTH + POOL_WIDTH + 2 * D_MODEL
D_FF = 4 * D_MODEL
N_MOD = 6
LN_EPS = 1e-5
ALPHA = 2.0 ** 0.25

SUBLANES = 8
LANES = 128
VMEM_LIMIT = 62 * 1024 * 1024

SSM_CHUNK = 128
SSM_ROWS = SUBLANES * SSM_CHUNK
SSM_GB = 4
SSM_CH = SSM_WIDTH // SSM_GB
SSM_ST = SSM_CH // SSM_GROUP * SSM_STATE
SSM_COLS = 512

BF16 = jnp.bfloat16
F32 = jnp.float32


def _cparams(sem):
    return pltpu.CompilerParams(dimension_semantics=sem, vmem_limit_bytes=VMEM_LIMIT)


def _ln(x):
    mu = jnp.mean(x, axis=-1, keepdims=True)
    xc = x - mu
    var = jnp.mean(xc * xc, axis=-1, keepdims=True)
    return xc * lax.rsqrt(var + LN_EPS)


def _mod_kernel(c_ref, w_ref, b_ref, o_ref):
    c = c_ref[...]
    c_act = (c * jax.nn.sigmoid(c)).astype(BF16)
    o_ref[...] = jnp.dot(c_act, w_ref[...].astype(BF16),
                         preferred_element_type=F32) + b_ref[...]


def _mod(c_pad, w_ada, b_ada):
    tn = 1024
    n = w_ada.shape[1]
    return pl.pallas_call(
        _mod_kernel,
        grid=(n // tn,),
        in_specs=[pl.BlockSpec((SUBLANES, D_MODEL), lambda j: (0, 0)),
                  pl.BlockSpec((D_MODEL, tn), lambda j: (0, j)),
                  pl.BlockSpec((1, tn), lambda j: (0, j))],
        out_specs=pl.BlockSpec((SUBLANES, tn), lambda j: (0, j)),
        out_shape=jax.ShapeDtypeStruct((SUBLANES, n), F32),
        compiler_params=_cparams(("arbitrary",)),
        name="mod",
    )(c_pad, w_ada, b_ada)


def _in_proj_kernel(x_ref, sh_ref, sc_ref, w_ref, proj_ref, h_ref):
    j = pl.program_id(1)
    half = x_ref.shape[0] // 2

    @pl.when(j == 0)
    def _():
        for s in range(2):
            rs = slice(s * half, (s + 1) * half)
            y = _ln(x_ref[rs, :])
            h_ref[rs, :] = (y * (1.0 + sc_ref[0]) + sh_ref[0]).astype(BF16)
            proj_ref[rs, :] = jnp.dot(h_ref[rs, :], w_ref[...],
                                      preferred_element_type=F32).astype(BF16)

    @pl.when(j != 0)
    def _():
        proj_ref[...] = jnp.dot(h_ref[...], w_ref[...],
                                preferred_element_type=F32).astype(BF16)


def _in_proj(x2, sh, sc, w_in, seq):
    tm, tn = 1024, 2048
    t = x2.shape[0]
    per_batch = seq // tm
    return pl.pallas_call(
        _in_proj_kernel,
        grid=(t // tm, IN_WIDTH // tn),
        in_specs=[pl.BlockSpec((tm, D_MODEL), lambda i, j: (i, 0)),
                  pl.BlockSpec((1, 1, D_MODEL), lambda i, j: (i // per_batch, 0, 0)),
                  pl.BlockSpec((1, 1, D_MODEL), lambda i, j: (i // per_batch, 0, 0)),
                  pl.BlockSpec((D_MODEL, tn), lambda i, j: (0, j))],
        out_specs=pl.BlockSpec((tm, tn), lambda i, j: (i, j)),
        out_shape=jax.ShapeDtypeStruct((t, IN_WIDTH), BF16),
        scratch_shapes=[pltpu.VMEM((tm, D_MODEL), BF16)],
        compiler_params=_cparams(("parallel", "arbitrary")),
        name="in_proj",
    )(x2, sh, sc, w_in)


def _ssm_prep_kernel(lr_ref, li_ref, ldt_ref, br_ref, bi_ref,
                     are_ref, aim_ref, alre_ref, alim_ref, bbre_ref, bbim_ref):
    lr = lr_ref[...]
    li = li_ref[...]
    dt = jnp.exp(ldt_ref[...])
    mag = jnp.exp(lr * dt)
    ang = li * dt
    ab_re = mag * jnp.cos(ang)
    ab_im = mag * jnp.sin(ang)
    are_ref[...] = ab_re
    aim_ref[...] = ab_im
    mag_l = jnp.exp(lr * dt * float(SSM_CHUNK))
    ang_l = ang * float(SSM_CHUNK)
    alre_ref[...] = mag_l * jnp.cos(ang_l)
    alim_ref[...] = mag_l * jnp.sin(ang_l)
    num_re = ab_re - 1.0
    num_im = ab_im
    den = lr * lr + li * li
    f_re = (num_re * lr + num_im * li) / den
    f_im = (num_im * lr - num_re * li) / den
    br = br_ref[...]
    bi = bi_ref[...]
    bbre_ref[...] = f_re * br - f_im * bi
    bbim_ref[...] = f_re * bi + f_im * br


def _ssm_prep(lam_re, lam_im, log_dt, b_re, b_im):
    rep = lambda a: jnp.repeat(a, SSM_GROUP, axis=0)
    lr = rep(lam_re)
    li = rep(lam_im)
    ldt = rep(jnp.broadcast_to(log_dt[:, None], (SSM_GROUPS, SSM_STATE)))
    br = b_re.transpose(0, 2, 1).reshape(SSM_WIDTH, SSM_STATE)
    bi = b_im.transpose(0, 2, 1).reshape(SSM_WIDTH, SSM_STATE)
    shp = jax.ShapeDtypeStruct((SSM_WIDTH, SSM_STATE), F32)
    return pl.pallas_call(
        _ssm_prep_kernel,
        out_shape=[shp] * 6,
        name="ssm_prep",
    )(lr, li, ldt, br, bi)


def _sublane_transpose(blocks):
    sub = lax.broadcasted_iota(jnp.int32, blocks[0].shape, 1)
    a = list(blocks)
    for s in (1, 2, 4):
        keep = (sub & s) == 0
        nxt = list(a)
        for r in range(SUBLANES):
            if r & s == 0:
                lo, hi = a[r], a[r + s]
                nxt[r] = jnp.where(keep, lo, pltpu.roll(hi, s, 1))
                nxt[r + s] = jnp.where(keep, pltpu.roll(lo, SUBLANES - s, 1), hi)
        a = nxt
    return a


def _ssm_kernel(u_ref, wb_ref, wct_ref, a_ref, d_ref, *refs, n_cast):
    cast_in = refs[:n_cast]
    z_ref = refs[n_cast]
    cast_out = refs[n_cast + 1:2 * n_cast + 1]
    up_ref, upb_ref, bu_ref, xb_ref, cst_ref, st_ref, zp_ref = refs[2 * n_cast + 1:]
    k = pl.program_id(2)
    L = SSM_CHUNK
    lh = L // SUBLANES

    for src, dst in zip(cast_in, cast_out):
        dst[...] = src[...].astype(BF16)

    @pl.when(k == 0)
    def _():
        cst_ref[...] = jnp.zeros_like(cst_ref)

    u4 = u_ref[...].astype(F32).reshape(SUBLANES, lh, SUBLANES, SSM_CH)
    for l_lo, blk in enumerate(_sublane_transpose([u4[r] for r in range(SUBLANES)])):
        up_ref[:, l_lo] = blk
    upb_ref[...] = up_ref[...].reshape(SSM_ROWS, SSM_CH).astype(BF16)

    for n in range(2 * SSM_ST // 256):
        bu_ref[:, n * 256:(n + 1) * 256] = jnp.dot(
            upb_ref[...], wb_ref[0, :, n * 256:(n + 1) * 256],
            preferred_element_type=F32)

    row_id = lax.broadcasted_iota(jnp.int32, (SUBLANES, SSM_COLS), 0)
    for cg in range(SSM_ST // SSM_COLS):
        re_sl = slice(cg * SSM_COLS, (cg + 1) * SSM_COLS)
        im_sl = slice(SSM_ST + cg * SSM_COLS, SSM_ST + (cg + 1) * SSM_COLS)
        ar = a_ref[0, 0, :, re_sl]
        ai = a_ref[1, 0, :, re_sl]

        def step(l, sr, si):
            row = pl.multiple_of(l * SUBLANES, SUBLANES)
            br = bu_ref[pl.ds(row, SUBLANES), re_sl]
            bi = bu_ref[pl.ds(row, SUBLANES), im_sl]
            return ar * sr - ai * si + br, ar * si + ai * sr + bi

        def pass1(l, carry):
            return step(l, *carry)
        er, ei = lax.fori_loop(0, L, pass1, (cst_ref[:, re_sl], cst_ref[:, im_sl]),
                               unroll=2)

        alr = a_ref[2, 0, :, re_sl]
        ali = a_ref[3, 0, :, re_sl]
        tr, ti = er, ei
        for r in range(1, SUBLANES):
            pr = pltpu.roll(tr, 1, 0)
            pi = pltpu.roll(ti, 1, 0)
            tr = jnp.where(row_id == r, er + alr * pr - ali * pi, tr)
            ti = jnp.where(row_id == r, ei + alr * pi + ali * pr, ti)
        pr = pltpu.roll(tr, 1, 0)
        pi = pltpu.roll(ti, 1, 0)
        first = row_id == 0
        st_ref[:, re_sl] = jnp.where(first, cst_ref[:, re_sl], pr)
        st_ref[:, im_sl] = jnp.where(first, cst_ref[:, im_sl], pi)
        cst_ref[:, re_sl] = jnp.where(first, pr, 0.0)
        cst_ref[:, im_sl] = jnp.where(first, pi, 0.0)

        def pass2(m, carry):
            sr, si = carry
            r1, i1 = step(2 * m, sr, si)
            r2, i2 = step(2 * m + 1, r1, i1)
            row = pl.multiple_of(m * 2 * SUBLANES, 2 * SUBLANES)
            xb_ref[pl.ds(row, 2 * SUBLANES), re_sl] = (
                jnp.concatenate([r1, r2], axis=0).astype(BF16))
            xb_ref[pl.ds(row, 2 * SUBLANES), im_sl] = (
                jnp.concatenate([i1, i2], axis=0).astype(BF16))
            return r2, i2
        lax.fori_loop(0, L // 2, pass2, (st_ref[:, re_sl], st_ref[:, im_sl]))

    mrows = 256
    mh = mrows // (SUBLANES * SUBLANES)
    for m in range(SSM_ROWS // mrows):
        rs = slice(m * mrows, (m + 1) * mrows)
        y = lax.dot_general(xb_ref[rs, :], wct_ref[0], (((1,), (1,)), ((), ())),
                            preferred_element_type=F32)
        y = y + d_ref[0] * up_ref[m * mh:(m + 1) * mh].reshape(mrows, SSM_CH)
        zp_ref[m * mh:(m + 1) * mh] = jax.nn.gelu(y, approximate=True).reshape(
            mh, SUBLANES, SUBLANES, SSM_CH)

    blocks = _sublane_transpose([zp_ref[:, l_lo] for l_lo in range(SUBLANES)])
    for r, blk in enumerate(blocks):
        z_ref[r * L:(r + 1) * L, :] = blk.reshape(L, SSM_CH).astype(BF16)


def _ssm(proj, wb, wct, a_all, d, weights, seq):
    t = proj.shape[0]
    nb = t // seq
    nk = seq // SSM_ROWS
    steps = SSM_GB * nb * nk
    tok = lambda g, b, k: (b * nk + k, g)
    par = lambda g, b, k: (g, 0, 0)
    step = lambda g, b, k: ((g * nb + b) * nk + k, 0)
    lh = SSM_CHUNK // SUBLANES
    cast_specs = [pl.BlockSpec((w.shape[0] // steps, w.shape[1]), step) for w in weights]
    return pl.pallas_call(
        functools.partial(_ssm_kernel, n_cast=len(weights)),
        grid=(SSM_GB, nb, nk),
        in_specs=[pl.BlockSpec((SSM_ROWS, SSM_CH), tok),
                  pl.BlockSpec((1, SSM_CH, 2 * SSM_ST), par),
                  pl.BlockSpec((1, SSM_CH, 2 * SSM_ST), par),
                  pl.BlockSpec((4, 1, SUBLANES, SSM_ST), lambda g, b, k: (0, g, 0, 0)),
                  pl.BlockSpec((1, 1, SSM_CH), par)] + cast_specs,
        out_specs=[pl.BlockSpec((SSM_ROWS, SSM_CH), tok)] + cast_specs,
        out_shape=[jax.ShapeDtypeStruct((t, SSM_WIDTH), BF16)]
        + [jax.ShapeDtypeStruct(w.shape, BF16) for w in weights],
        scratch_shapes=[pltpu.VMEM((lh, SUBLANES, SUBLANES, SSM_CH), F32),
                        pltpu.VMEM((SSM_ROWS, SSM_CH), BF16),
                        pltpu.VMEM((SSM_ROWS, 2 * SSM_ST), F32),
                        pltpu.VMEM((SSM_ROWS, 2 * SSM_ST), BF16),
                        pltpu.VMEM((SUBLANES, 2 * SSM_ST), F32),
                        pltpu.VMEM((SUBLANES, 2 * SSM_ST), F32),
                        pltpu.VMEM((lh, SUBLANES, SUBLANES, SSM_CH), F32)],
        compiler_params=_cparams(("arbitrary", "arbitrary", "arbitrary")),
        name="ssm",
    )(proj, wb, wct, a_all, d, *weights)


POOL_HALO = 16


def _pool_kernel(u_ref, w_ref, s_ref, o_ref, ext_ref, *, tt):
    k = pl.program_id(1)

    @pl.when(k == 0)
    def _():
        ext_ref[0:POOL_HALO, :] = jnp.zeros((POOL_HALO, POOL_WIDTH), F32)

    @pl.when(k != 0)
    def _():
        ext_ref[0:POOL_HALO, :] = ext_ref[tt:tt + POOL_HALO, :]

    ext_ref[POOL_HALO:POOL_HALO + tt, :] = u_ref[...].astype(F32)
    pos = (k * tt + lax.broadcasted_iota(jnp.int32, (tt, POOL_GROUP), 0)).astype(F32)
    for gi, w in enumerate(POOL_WINDOWS):
        cs = slice(gi * POOL_GROUP, (gi + 1) * POOL_GROUP)
        cur = ext_ref[POOL_HALO:POOL_HALO + tt, cs]
        tot = cur
        for j in range(1, w):
            tot = tot + ext_ref[POOL_HALO - j:POOL_HALO - j + tt, cs]
        count = jnp.minimum(pos + 1.0, float(w))
        pooled = tot / count - cur
        y = jnp.dot(pooled.astype(BF16), w_ref[gi], preferred_element_type=F32)
        o_ref[:, cs] = (y * s_ref[:, cs]).astype(BF16)


def _pool(proj, w_pool, pool_scale, seq):
    tt = 512
    t = proj.shape[0]
    nk = seq // tt
    return pl.pallas_call(
        functools.partial(_pool_kernel, tt=tt),
        grid=(t // seq, nk),
        in_specs=[pl.BlockSpec((tt, POOL_WIDTH), lambda b, k: (b * nk + k, 1)),
                  pl.BlockSpec((len(POOL_WINDOWS), POOL_GROUP, POOL_GROUP),
                               lambda b, k: (0, 0, 0)),
                  pl.BlockSpec((1, POOL_WIDTH), lambda b, k: (0, 0))],
        out_specs=pl.BlockSpec((tt, POOL_WIDTH), lambda b, k: (b * nk + k, 0)),
        out_shape=jax.ShapeDtypeStruct((t, POOL_WIDTH), BF16),
        scratch_shapes=[pltpu.VMEM((tt + POOL_HALO, POOL_WIDTH), F32)],
        compiler_params=_cparams(("parallel", "arbitrary")),
        name="pool",
    )(proj, w_pool, pool_scale)


def _merge_kernel(z_ref, yp_ref, ga_ref, gb_ref, wv_ref, wg_ref, wp_ref, o_ref):
    half = z_ref.shape[0] // 2
    for s in range(2):
        rs = slice(s * half, (s + 1) * half)
        z = z_ref[rs, :]
        val = jnp.dot(z, wv_ref[...], preferred_element_type=F32)
        gate = jnp.dot(z, wg_ref[...], preferred_element_type=F32)
        y_a = val * jax.nn.sigmoid(gate)
        y_b = jnp.dot(yp_ref[rs, :], wp_ref[...], preferred_element_type=F32)
        merged = (jax.nn.sigmoid(ga_ref[rs, :].astype(F32)) * y_a
                  + jax.nn.sigmoid(gb_ref[rs, :].astype(F32)) * y_b)
        o_ref[rs, :] = merged.astype(BF16)


def _merge(z, yp, proj, w_val, w_gate, w_po):
    tm, tn = 1024, 1024
    t = z.shape[0]
    ga0 = (SSM_WIDTH + POOL_WIDTH) // tn
    gb0 = ga0 + D_MODEL // tn
    wspec = pl.BlockSpec((SSM_WIDTH, tn), lambda i, j: (0, j))
    return pl.pallas_call(
        _merge_kernel,
        grid=(t // tm, D_MODEL // tn),
        in_specs=[pl.BlockSpec((tm, SSM_WIDTH), lambda i, j: (i, 0)),
                  pl.BlockSpec((tm, POOL_WIDTH), lambda i, j: (i, 0)),
                  pl.BlockSpec((tm, tn), lambda i, j: (i, ga0 + j)),
                  pl.BlockSpec((tm, tn), lambda i, j: (i, gb0 + j)),
                  wspec, wspec, wspec],
        out_specs=pl.BlockSpec((tm, tn), lambda i, j: (i, j)),
        out_shape=jax.ShapeDtypeStruct((t, D_MODEL), BF16),
        compiler_params=_cparams(("parallel", "arbitrary")),
        name="merge",
    )(z, yp, proj, proj, w_val, w_gate, w_po)


def _out_ln_kernel(m_ref, w_ref, x_ref, g1_ref, lg_ref, lb_ref, sh_ref, sc_ref,
                   x1_ref, h2_ref):
    half = m_ref.shape[0] // 2
    for s in range(2):
        rs = slice(s * half, (s + 1) * half)
        y = jnp.dot(m_ref[rs, :], w_ref[...], preferred_element_type=F32)
        r = ALPHA * x_ref[rs, :] + g1_ref[0] * y
        x1 = _ln(r) * lg_ref[...] + lb_ref[...]
        x1_ref[rs, :] = x1
        h2_ref[rs, :] = (_ln(x1) * (1.0 + sc_ref[0]) + sh_ref[0]).astype(BF16)


def _out_ln(merged, w_out, x2, g1, ln_g, ln_b, sh2, sc2, seq):
    tm = 512
    t = x2.shape[0]
    per_batch = seq // tm
    row = pl.BlockSpec((tm, D_MODEL), lambda i: (i, 0))
    vec = pl.BlockSpec((1, D_MODEL), lambda i: (0, 0))
    bvec = pl.BlockSpec((1, 1, D_MODEL), lambda i: (i // per_batch, 0, 0))
    return pl.pallas_call(
        _out_ln_kernel,
        grid=(t // tm,),
        in_specs=[row,
                  pl.BlockSpec((D_MODEL, D_MODEL), lambda i: (0, 0),
                               pipeline_mode=pl.Buffered(1)),
                  row, bvec, vec, vec, bvec, bvec],
        out_specs=[row, row],
        out_shape=[jax.ShapeDtypeStruct((t, D_MODEL), F32),
                   jax.ShapeDtypeStruct((t, D_MODEL), BF16)],
        compiler_params=_cparams(("parallel",)),
        name="out_ln",
    )(merged, w_out, x2, g1, ln_g, ln_b, sh2, sc2)


def _mlp_kernel(h_ref, w1_ref, w2_ref, x1_ref, g2_ref, lg_ref, lb_ref, o_ref):
    f = pl.program_id(1)
    last = pl.num_programs(1) - 1
    half = h_ref.shape[0] // 2

    def ff(rows):
        a = jnp.maximum(jnp.dot(h_ref[rows, :], w1_ref[...],
                                preferred_element_type=F32), 0.0)
        return jnp.dot((a * a).astype(BF16), w2_ref[...], preferred_element_type=F32)

    @pl.when(f == 0)
    def _():
        o_ref[...] = ff(slice(None))

    @pl.when(jnp.logical_and(f > 0, f < last))
    def _():
        o_ref[...] += ff(slice(None))

    @pl.when(f == last)
    def _():
        for s in range(2):
            rs = slice(s * half, (s + 1) * half)
            r = ALPHA * x1_ref[rs, :] + g2_ref[0] * (o_ref[rs, :] + ff(rs))
            o_ref[rs, :] = _ln(r) * lg_ref[...] + lb_ref[...]


def _mlp(h2, w1, w2, x1, g2, ln_g, ln_b, seq):
    tm, tf = 1024, 1024
    t = h2.shape[0]
    per_batch = seq // tm
    once = pl.Buffered(1)
    vec = pl.BlockSpec((1, D_MODEL), lambda i, f: (0, 0))
    return pl.pallas_call(
        _mlp_kernel,
        grid=(t // tm, D_FF // tf),
        in_specs=[pl.BlockSpec((tm, D_MODEL), lambda i, f: (i, 0), pipeline_mode=once),
                  pl.BlockSpec((D_MODEL, tf), lambda i, f: (0, f)),
                  pl.BlockSpec((tf, D_MODEL), lambda i, f: (f, 0)),
                  pl.BlockSpec((tm, D_MODEL), lambda i, f: (i, 0), pipeline_mode=once),
                  pl.BlockSpec((1, 1, D_MODEL), lambda i, f: (i // per_batch, 0, 0)),
                  vec, vec],
        out_specs=pl.BlockSpec((tm, D_MODEL), lambda i, f: (i, 0)),
        out_shape=jax.ShapeDtypeStruct((t, D_MODEL), F32),
        compiler_params=_cparams(("parallel", "arbitrary")),
        name="mlp",
    )(h2, w1, w2, x1, g2, ln_g, ln_b)


def _block_diag_weights(bb_re, bb_im, c_re, c_im):
    ng = SSM_CH // SSM_GROUP
    rows = jnp.arange(SSM_CH)[:, None] // SSM_GROUP
    cols = jnp.arange(SSM_ST)[None, :] // SSM_STATE
    on_diag = rows == cols

    def blocks(w):
        w = w.reshape(SSM_GB, SSM_CH, SSM_STATE)
        return jnp.where(on_diag, jnp.tile(w, (1, 1, ng)), 0.0)

    c_re = c_re.reshape(SSM_WIDTH, SSM_STATE)
    c_im = c_im.reshape(SSM_WIDTH, SSM_STATE)
    wb = jnp.concatenate([blocks(bb_re), blocks(bb_im)], axis=-1).astype(BF16)
    wct = jnp.concatenate([blocks(c_re), blocks(-c_im)], axis=-1).astype(BF16)
    return wb, wct


def kernel(x, c, w_ada, b_ada, w_in, lam_re, lam_im, log_dt, ssm_b_re, ssm_b_im, ssm_c_re, ssm_c_im, ssm_d, w_glu_val, w_glu_gate, w_pool, pool_scale, w_pool_out, w_out, ln1_g, ln1_b, w_ff1, w_ff2, ln2_g, ln2_b):
    bsz, seq, d = x.shape
    t = bsz * seq
    l = 0
    x2 = x.reshape(t, d)

    c_pad = jnp.zeros((SUBLANES, d), F32).at[:bsz].set(c)
    mod = _mod(c_pad, w_ada[l], b_ada[l][None, :])[:bsz]
    sh1, sc1, g1, sh2, sc2, g2 = [m[:, None, :] for m in jnp.split(mod, N_MOD, axis=-1)]

    proj = _in_proj(x2, sh1, sc1, w_in[l].astype(BF16), seq)

    a_re, a_im, al_re, al_im, bb_re, bb_im = _ssm_prep(
        lam_re[l], lam_im[l], log_dt[l], ssm_b_re[l], ssm_b_im[l])
    wb, wct = _block_diag_weights(bb_re, bb_im, ssm_c_re[l], ssm_c_im[l])
    a_all = jnp.stack([a_re, a_im, al_re, al_im])[:, ::SSM_GROUP]
    a_all = jnp.broadcast_to(a_all.reshape(4, SSM_GB, 1, SSM_ST),
                             (4, SSM_GB, SUBLANES, SSM_ST))

    later = [w_ff1[l], w_ff2[l], w_out[l], w_glu_val[l], w_glu_gate[l], w_pool_out[l]]
    z, w1_bf, w2_bf, wo_bf, wv_bf, wg_bf, wp_bf = _ssm(
        proj, wb, wct, a_all, ssm_d[l].reshape(SSM_GB, 1, SSM_CH), later, seq)

    yp = _pool(proj, w_pool[l].astype(BF16), pool_scale[l][None, :], seq)

    merged = _merge(z, yp, proj, wv_bf, wg_bf, wp_bf)

    x1, h2 = _out_ln(merged, wo_bf, x2, g1, ln1_g[l][None, :],
                     ln1_b[l][None, :], sh2, sc2, seq)

    out = _mlp(h2, w1_bf, w2_bf, x1, g2,
               ln2_g[l][None, :], ln2_b[l][None, :], seq)
    return out.reshape(bsz, seq, d)
```

```python
import functools
import math

import jax
import jax.numpy as jnp
from jax import lax
from jax.experimental import pallas as pl
from jax.experimental.pallas import tpu as pltpu

D_MODEL = 2048
SSM_WIDTH = 1024
SSM_GROUP = 16
SSM_GROUPS = 64
SSM_STATE = 64
POOL_WIDTH = 1024
POOL_WINDOWS = (2, 4, 8, 16)
POOL_GROUP = 256
IN_WIDTH = SSM_WIDTH + POOL_WIDTH + 2 * D_MODEL
D_FF = 4 * D_MODEL
N_MOD = 6
LN_EPS = 1e-5
ALPHA = 2.0 ** 0.25

SUBLANES = 8
LANES = 128
VMEM_LIMIT = 62 * 1024 * 1024

SSM_CHUNK = 128
SSM_ROWS = SUBLANES * SSM_CHUNK
SSM_GB = 4
SSM_CH = SSM_WIDTH // SSM_GB
SSM_ST = SSM_CH // SSM_GROUP * SSM_STATE
SSM_COLS = 512

BF16 = jnp.bfloat16
F32 = jnp.float32


def _cparams(sem):
    return pltpu.CompilerParams(dimension_semantics=sem, vmem_limit_bytes=VMEM_LIMIT)


def _ln(x):
    mu = jnp.mean(x, axis=-1, keepdims=True)
    xc = x - mu
    var = jnp.mean(xc * xc, axis=-1, keepdims=True)
    return xc * lax.rsqrt(var + LN_EPS)


def _mod_kernel(c_ref, w_ref, b_ref, o_ref):
    c = c_ref[...]
    c_act = (c * jax.nn.sigmoid(c)).astype(BF16)
    o_ref[...] = jnp.dot(c_act, w_ref[...].astype(BF16),
                         preferred_element_type=F32) + b_ref[...]


def _mod(c_pad, w_ada, b_ada):
    tn = 1024
    n = w_ada.shape[1]
    return pl.pallas_call(
        _mod_kernel,
        grid=(n // tn,),
        in_specs=[pl.BlockSpec((SUBLANES, D_MODEL), lambda j: (0, 0)),
                  pl.BlockSpec((D_MODEL, tn), lambda j: (0, j)),
                  pl.BlockSpec((1, tn), lambda j: (0, j))],
        out_specs=pl.BlockSpec((SUBLANES, tn), lambda j: (0, j)),
        out_shape=jax.ShapeDtypeStruct((SUBLANES, n), F32),
        compiler_params=_cparams(("arbitrary",)),
        name="mod",
    )(c_pad, w_ada, b_ada)


def _in_proj_kernel(x_ref, sh_ref, sc_ref, w_ref, proj_ref, h_ref):
    j = pl.program_id(1)
    half = x_ref.shape[0] // 2

    @pl.when(j == 0)
    def _():
        for s in range(2):
            rs = slice(s * half, (s + 1) * half)
            y = _ln(x_ref[rs, :])
            h_ref[rs, :] = (y * (1.0 + sc_ref[0]) + sh_ref[0]).astype(BF16)
            proj_ref[rs, :] = jnp.dot(h_ref[rs, :], w_ref[...],
                                      preferred_element_type=F32).astype(BF16)

    @pl.when(j != 0)
    def _():
        proj_ref[...] = jnp.dot(h_ref[...], w_ref[...],
                                preferred_element_type=F32).astype(BF16)


def _in_proj(x2, sh, sc, w_in, seq):
    tm, tn = 1024, 2048
    t = x2.shape[0]
    per_batch = seq // tm
    return pl.pallas_call(
        _in_proj_kernel,
        grid=(t // tm, IN_WIDTH // tn),
        in_specs=[pl.BlockSpec((tm, D_MODEL), lambda i, j: (i, 0)),
                  pl.BlockSpec((1, 1, D_MODEL), lambda i, j: (i // per_batch, 0, 0)),
                  pl.BlockSpec((1, 1, D_MODEL), lambda i, j: (i // per_batch, 0, 0)),
                  pl.BlockSpec((D_MODEL, tn), lambda i, j: (0, j))],
        out_specs=pl.BlockSpec((tm, tn), lambda i, j: (i, j)),
        out_shape=jax.ShapeDtypeStruct((t, IN_WIDTH), BF16),
        scratch_shapes=[pltpu.VMEM((tm, D_MODEL), BF16)],
        compiler_params=_cparams(("parallel", "arbitrary")),
        name="in_proj",
    )(x2, sh, sc, w_in)


def _ssm_prep_kernel(lr_ref, li_ref, ldt_ref, br_ref, bi_ref,
                     are_ref, aim_ref, alre_ref, alim_ref, bbre_ref, bbim_ref):
    lr = lr_ref[...]
    li = li_ref[...]
    dt = jnp.exp(ldt_ref[...])
    mag = jnp.exp(lr * dt)
    ang = li * dt
    ab_re = mag * jnp.cos(ang)
    ab_im = mag * jnp.sin(ang)
    are_ref[...] = ab_re
    aim_ref[...] = ab_im
    mag_l = jnp.exp(lr * dt * float(SSM_CHUNK))
    ang_l = ang * float(SSM_CHUNK)
    alre_ref[...] = mag_l * jnp.cos(ang_l)
    alim_ref[...] = mag_l * jnp.sin(ang_l)
    num_re = ab_re - 1.0
    num_im = ab_im
    den = lr * lr + li * li
    f_re = (num_re * lr + num_im * li) / den
    f_im = (num_im * lr - num_re * li) / den
    br = br_ref[...]
    bi = bi_ref[...]
    bbre_ref[...] = f_re * br - f_im * bi
    bbim_ref[...] = f_re * bi + f_im * br


def _ssm_prep(lam_re, lam_im, log_dt, b_re, b_im):
    rep = lambda a: jnp.repeat(a, SSM_GROUP, axis=0)
    lr = rep(lam_re)
    li = rep(lam_im)
    ldt = rep(jnp.broadcast_to(log_dt[:, None], (SSM_GROUPS, SSM_STATE)))
    br = b_re.transpose(0, 2, 1).reshape(SSM_WIDTH, SSM_STATE)
    bi = b_im.transpose(0, 2, 1).reshape(SSM_WIDTH, SSM_STATE)
    shp = jax.ShapeDtypeStruct((SSM_WIDTH, SSM_STATE), F32)
    return pl.pallas_call(
        _ssm_prep_kernel,
        out_shape=[shp] * 6,
        name="ssm_prep",
    )(lr, li, ldt, br, bi)


def _sublane_transpose(blocks):
    sub = lax.broadcasted_iota(jnp.int32, blocks[0].shape, 1)
    a = list(blocks)
    for s in (1, 2, 4):
        keep = (sub & s) == 0
        nxt = list(a)
        for r in range(SUBLANES):
            if r & s == 0:
                lo, hi = a[r], a[r + s]
                nxt[r] = jnp.where(keep, lo, pltpu.roll(hi, s, 1))
                nxt[r + s] = jnp.where(keep, pltpu.roll(lo, SUBLANES - s, 1), hi)
        a = nxt
    return a


def _ssm_kernel(u_ref, wb_ref, wct_ref, a_ref, d_ref, *refs, n_cast):
    cast_in = refs[:n_cast]
    z_ref = refs[n_cast]
    cast_out = refs[n_cast + 1:2 * n_cast + 1]
    up_ref, upb_ref, bu_ref, xb_ref, cst_ref, st_ref, zp_ref = refs[2 * n_cast + 1:]
    k = pl.program_id(2)
    L = SSM_CHUNK
    lh = L // SUBLANES

    for src, dst in zip(cast_in, cast_out):
        dst[...] = src[...].astype(BF16)

    @pl.when(k == 0)
    def _():
        cst_ref[...] = jnp.zeros_like(cst_ref)

    u4 = u_ref[...].astype(F32).reshape(SUBLANES, lh, SUBLANES, SSM_CH)
    for l_lo, blk in enumerate(_sublane_transpose([u4[r] for r in range(SUBLANES)])):
        up_ref[:, l_lo] = blk
    upb_ref[...] = up_ref[...].reshape(SSM_ROWS, SSM_CH).astype(BF16)

    for n in range(2 * SSM_ST // 256):
        bu_ref[:, n * 256:(n + 1) * 256] = jnp.dot(
            upb_ref[...], wb_ref[0, :, n * 256:(n + 1) * 256],
            preferred_element_type=F32)

    row_id = lax.broadcasted_iota(jnp.int32, (SUBLANES, SSM_COLS), 0)
    for cg in range(SSM_ST // SSM_COLS):
        re_sl = slice(cg * SSM_COLS, (cg + 1) * SSM_COLS)
        im_sl = slice(SSM_ST + cg * SSM_COLS, SSM_ST + (cg + 1) * SSM_COLS)
        ar = a_ref[0, 0, :, re_sl]
        ai = a_ref[1, 0, :, re_sl]

        def step(l, sr, si):
            row = pl.multiple_of(l * SUBLANES, SUBLANES)
            br = bu_ref[pl.ds(row, SUBLANES), re_sl]
            bi = bu_ref[pl.ds(row, SUBLANES), im_sl]
            return ar * sr - ai * si + br, ar * si + ai * sr + bi

        def pass1(l, carry):
            return step(l, *carry)
        er, ei = lax.fori_loop(0, L, pass1, (cst_ref[:, re_sl], cst_ref[:, im_sl]),
                               unroll=2)

        alr = a_ref[2, 0, :, re_sl]
        ali = a_ref[3, 0, :, re_sl]
        tr, ti = er, ei
        for r in range(1, SUBLANES):
            pr = pltpu.roll(tr, 1, 0)
            pi = pltpu.roll(ti, 1, 0)
            tr = jnp.where(row_id == r, er + alr * pr - ali * pi, tr)
            ti = jnp.where(row_id == r, ei + alr * pi + ali * pr, ti)
        pr = pltpu.roll(tr, 1, 0)
        pi = pltpu.roll(ti, 1, 0)
        first = row_id == 0
        st_ref[:, re_sl] = jnp.where(first, cst_ref[:, re_sl], pr)
        st_ref[:, im_sl] = jnp.where(first, cst_ref[:, im_sl], pi)
        cst_ref[:, re_sl] = jnp.where(first, pr, 0.0)
        cst_ref[:, im_sl] = jnp.where(first, pi, 0.0)

        def pass2(m, carry):
            sr, si = carry
            r1, i1 = step(2 * m, sr, si)
            r2, i2 = step(2 * m + 1, r1, i1)
            row = pl.multiple_of(m * 2 * SUBLANES, 2 * SUBLANES)
            xb_ref[pl.ds(row, 2 * SUBLANES), re_sl] = (
                jnp.concatenate([r1, r2], axis=0).astype(BF16))
            xb_ref[pl.ds(row, 2 * SUBLANES), im_sl] = (
                jnp.concatenate([i1, i2], axis=0).astype(BF16))
            return r2, i2
        lax.fori_loop(0, L // 2, pass2, (st_ref[:, re_sl], st_ref[:, im_sl]),
                      unroll=2)

    mrows = 256
    mh = mrows // (SUBLANES * SUBLANES)
    for m in range(SSM_ROWS // mrows):
        rs = slice(m * mrows, (m + 1) * mrows)
        y = lax.dot_general(xb_ref[rs, :], wct_ref[0], (((1,), (1,)), ((), ())),
                            preferred_element_type=F32)
        y = y + d_ref[0] * up_ref[m * mh:(m + 1) * mh].reshape(mrows, SSM_CH)
        zp_ref[m * mh:(m + 1) * mh] = jax.nn.gelu(y, approximate=True).reshape(
            mh, SUBLANES, SUBLANES, SSM_CH)

    blocks = _sublane_transpose([zp_ref[:, l_lo] for l_lo in range(SUBLANES)])
    for r, blk in enumerate(blocks):
        z_ref[r * L:(r + 1) * L, :] = blk.reshape(L, SSM_CH).astype(BF16)


def _ssm(proj, wb, wct, a_all, d, weights, seq):
    t = proj.shape[0]
    nb = t // seq
    nk = seq // SSM_ROWS
    steps = SSM_GB * nb * nk
    tok = lambda g, b, k: (b * nk + k, g)
    par = lambda g, b, k: (g, 0, 0)
    step = lambda g, b, k: ((g * nb + b) * nk + k, 0)
    lh = SSM_CHUNK // SUBLANES
    cast_specs = [pl.BlockSpec((w.shape[0] // steps, w.shape[1]), step) for w in weights]
    return pl.pallas_call(
        functools.partial(_ssm_kernel, n_cast=len(weights)),
        grid=(SSM_GB, nb, nk),
        in_specs=[pl.BlockSpec((SSM_ROWS, SSM_CH), tok),
                  pl.BlockSpec((1, SSM_CH, 2 * SSM_ST), par),
                  pl.BlockSpec((1, SSM_CH, 2 * SSM_ST), par),
                  pl.BlockSpec((4, 1, SUBLANES, SSM_ST), lambda g, b, k: (0, g, 0, 0)),
                  pl.BlockSpec((1, 1, SSM_CH), par)] + cast_specs,
        out_specs=[pl.BlockSpec((SSM_ROWS, SSM_CH), tok)] + cast_specs,
        out_shape=[jax.ShapeDtypeStruct((t, SSM_WIDTH), BF16)]
        + [jax.ShapeDtypeStruct(w.shape, BF16) for w in weights],
        scratch_shapes=[pltpu.VMEM((lh, SUBLANES, SUBLANES, SSM_CH), F32),
                        pltpu.VMEM((SSM_ROWS, SSM_CH), BF16),
                        pltpu.VMEM((SSM_ROWS, 2 * SSM_ST), F32),
                        pltpu.VMEM((SSM_ROWS, 2 * SSM_ST), BF16),
                        pltpu.VMEM((SUBLANES, 2 * SSM_ST), F32),
                        pltpu.VMEM((SUBLANES, 2 * SSM_ST), F32),
                        pltpu.VMEM((lh, SUBLANES, SUBLANES, SSM_CH), F32)],
        compiler_params=_cparams(("arbitrary", "arbitrary", "arbitrary")),
        name="ssm",
    )(proj, wb, wct, a_all, d, *weights)


POOL_HALO = 32
POOL_LEAD = 16


def _pool_kernel(u_ref, w_ref, s_ref, o_ref, ext_ref, sum_ref, *, tt):
    k = pl.program_id(1)
    base = POOL_HALO - POOL_LEAD
    n = tt + POOL_LEAD

    @pl.when(k == 0)
    def _():
        ext_ref[0:POOL_HALO, :] = jnp.zeros((POOL_HALO, POOL_WIDTH), F32)

    @pl.when(k != 0)
    def _():
        ext_ref[0:POOL_HALO, :] = ext_ref[tt:tt + POOL_HALO, :]

    ext_ref[POOL_HALO:POOL_HALO + tt, :] = u_ref[...].astype(F32)
    sum_ref[0:base, :] = jnp.zeros((base, POOL_GROUP), F32)
    pos = (k * tt + lax.broadcasted_iota(jnp.int32, (tt, POOL_GROUP), 0)).astype(F32)
    for gi, w in enumerate(POOL_WINDOWS):
        cs = slice(gi * POOL_GROUP, (gi + 1) * POOL_GROUP)
        tot = ext_ref[base:base + n, cs] + ext_ref[base - 1:base - 1 + n, cs]
        m = 2
        while m < w:
            sum_ref[base:base + n, :] = tot
            tot = tot + sum_ref[base - m:base - m + n, :]
            m *= 2
        cur = ext_ref[POOL_HALO:POOL_HALO + tt, cs]
        count = jnp.minimum(pos + 1.0, float(w))
        pooled = tot[POOL_LEAD:, :] / count - cur
        y = jnp.dot(pooled.astype(BF16), w_ref[gi], preferred_element_type=F32)
        o_ref[:, cs] = (y * s_ref[:, cs]).astype(BF16)


def _pool(proj, w_pool, pool_scale, seq):
    tt = 512
    t = proj.shape[0]
    nk = seq // tt
    return pl.pallas_call(
        functools.partial(_pool_kernel, tt=tt),
        grid=(t // seq, nk),
        in_specs=[pl.BlockSpec((tt, POOL_WIDTH), lambda b, k: (b * nk + k, 1)),
                  pl.BlockSpec((len(POOL_WINDOWS), POOL_GROUP, POOL_GROUP),
                               lambda b, k: (0, 0, 0)),
                  pl.BlockSpec((1, POOL_WIDTH), lambda b, k: (0, 0))],
        out_specs=pl.BlockSpec((tt, POOL_WIDTH), lambda b, k: (b * nk + k, 0)),
        out_shape=jax.ShapeDtypeStruct((t, POOL_WIDTH), BF16),
        scratch_shapes=[pltpu.VMEM((tt + POOL_HALO, POOL_WIDTH), F32),
                        pltpu.VMEM((tt + POOL_HALO, POOL_GROUP), F32)],
        compiler_params=_cparams(("parallel", "arbitrary")),
        name="pool",
    )(proj, w_pool, pool_scale)


def _merge_kernel(z_ref, yp_ref, ga_ref, gb_ref, wv_ref, wg_ref, wp_ref, o_ref):
    half = z_ref.shape[0] // 2
    for s in range(2):
        rs = slice(s * half, (s + 1) * half)
        z = z_ref[rs, :]
        val = jnp.dot(z, wv_ref[...], preferred_element_type=F32)
        gate = jnp.dot(z, wg_ref[...], preferred_element_type=F32)
        y_a = val * jax.nn.sigmoid(gate)
        y_b = jnp.dot(yp_ref[rs, :], wp_ref[...], preferred_element_type=F32)
        merged = (jax.nn.sigmoid(ga_ref[rs, :].astype(F32)) * y_a
                  + jax.nn.sigmoid(gb_ref[rs, :].astype(F32)) * y_b)
        o_ref[rs, :] = merged.astype(BF16)


def _merge(z, yp, proj, w_val, w_gate, w_po):
    tm, tn = 1024, 1024
    t = z.shape[0]
    ga0 = (SSM_WIDTH + POOL_WIDTH) // tn
    gb0 = ga0 + D_MODEL // tn
    wspec = pl.BlockSpec((SSM_WIDTH, tn), lambda i, j: (0, j))
    return pl.pallas_call(
        _merge_kernel,
        grid=(t // tm, D_MODEL // tn),
        in_specs=[pl.BlockSpec((tm, SSM_WIDTH), lambda i, j: (i, 0)),
                  pl.BlockSpec((tm, POOL_WIDTH), lambda i, j: (i, 0)),
                  pl.BlockSpec((tm, tn), lambda i, j: (i, ga0 + j)),
                  pl.BlockSpec((tm, tn), lambda i, j: (i, gb0 + j)),
                  wspec, wspec, wspec],
        out_specs=pl.BlockSpec((tm, tn), lambda i, j: (i, j)),
        out_shape=jax.ShapeDtypeStruct((t, D_MODEL), BF16),
        compiler_params=_cparams(("parallel", "arbitrary")),
        name="merge",
    )(z, yp, proj, proj, w_val, w_gate, w_po)


def _out_ln_kernel(m_ref, w_ref, x_ref, g1_ref, lg_ref, lb_ref, sh_ref, sc_ref,
                   x1_ref, h2_ref):
    half = m_ref.shape[0] // 2
    for s in range(2):
        rs = slice(s * half, (s + 1) * half)
        y = jnp.dot(m_ref[rs, :], w_ref[...], preferred_element_type=F32)
        r = ALPHA * x_ref[rs, :] + g1_ref[0] * y
        x1 = _ln(r) * lg_ref[...] + lb_ref[...]
        x1_ref[rs, :] = x1
        h2_ref[rs, :] = (_ln(x1) * (1.0 + sc_ref[0]) + sh_ref[0]).astype(BF16)


def _out_ln(merged, w_out, x2, g1, ln_g, ln_b, sh2, sc2, seq):
    tm = 512
    t = x2.shape[0]
    per_batch = seq // tm
    row = pl.BlockSpec((tm, D_MODEL), lambda i: (i, 0))
    vec = pl.BlockSpec((1, D_MODEL), lambda i: (0, 0))
    bvec = pl.BlockSpec((1, 1, D_MODEL), lambda i: (i // per_batch, 0, 0))
    return pl.pallas_call(
        _out_ln_kernel,
        grid=(t // tm,),
        in_specs=[row,
                  pl.BlockSpec((D_MODEL, D_MODEL), lambda i: (0, 0),
                               pipeline_mode=pl.Buffered(1)),
                  row, bvec, vec, vec, bvec, bvec],
        out_specs=[row, row],
        out_shape=[jax.ShapeDtypeStruct((t, D_MODEL), F32),
                   jax.ShapeDtypeStruct((t, D_MODEL), BF16)],
        compiler_params=_cparams(("parallel",)),
        name="out_ln",
    )(merged, w_out, x2, g1, ln_g, ln_b, sh2, sc2)


def _mlp_kernel(h_ref, w1_ref, w2_ref, x1_ref, g2_ref, lg_ref, lb_ref, o_ref):
    f = pl.program_id(1)
    last = pl.num_programs(1) - 1
    half = h_ref.shape[0] // 2

    def ff(rows):
        a = jnp.maximum(jnp.dot(h_ref[rows, :], w1_ref[...],
                                preferred_element_type=F32), 0.0)
        return jnp.dot((a * a).astype(BF16), w2_ref[...], preferred_element_type=F32)

    @pl.when(f == 0)
    def _():
        o_ref[...] = ff(slice(None))

    @pl.when(jnp.logical_and(f > 0, f < last))
    def _():
        o_ref[...] += ff(slice(None))

    @pl.when(f == last)
    def _():
        for s in range(2):
            rs = slice(s * half, (s + 1) * half)
            r = ALPHA * x1_ref[rs, :] + g2_ref[0] * (o_ref[rs, :] + ff(rs))
            o_ref[rs, :] = _ln(r) * lg_ref[...] + lb_ref[...]


def _mlp(h2, w1, w2, x1, g2, ln_g, ln_b, seq):
    tm, tf = 1024, 1024
    t = h2.shape[0]
    per_batch = seq // tm
    once = pl.Buffered(1)
    vec = pl.BlockSpec((1, D_MODEL), lambda i, f: (0, 0))
    return pl.pallas_call(
        _mlp_kernel,
        grid=(t // tm, D_FF // tf),
        in_specs=[pl.BlockSpec((tm, D_MODEL), lambda i, f: (i, 0), pipeline_mode=once),
                  pl.BlockSpec((D_MODEL, tf), lambda i, f: (0, f)),
                  pl.BlockSpec((tf, D_MODEL), lambda i, f: (f, 0)),
                  pl.BlockSpec((tm, D_MODEL), lambda i, f: (i, 0), pipeline_mode=once),
                  pl.BlockSpec((1, 1, D_MODEL), lambda i, f: (i // per_batch, 0, 0)),
                  vec, vec],
        out_specs=pl.BlockSpec((tm, D_MODEL), lambda i, f: (i, 0)),
        out_shape=jax.ShapeDtypeStruct((t, D_MODEL), F32),
        compiler_params=_cparams(("parallel", "arbitrary")),
        name="mlp",
    )(h2, w1, w2, x1, g2, ln_g, ln_b)


def _block_diag_weights(bb_re, bb_im, c_re, c_im):
    ng = SSM_CH // SSM_GROUP
    rows = jnp.arange(SSM_CH)[:, None] // SSM_GROUP
    cols = jnp.arange(SSM_ST)[None, :] // SSM_STATE
    on_diag = rows == cols

    def blocks(w):
        w = w.reshape(SSM_GB, SSM_CH, SSM_STATE)
        return jnp.where(on_diag, jnp.tile(w, (1, 1, ng)), 0.0)

    c_re = c_re.reshape(SSM_WIDTH, SSM_STATE)
    c_im = c_im.reshape(SSM_WIDTH, SSM_STATE)
    wb = jnp.concatenate([blocks(bb_re), blocks(bb_im)], axis=-1).astype(BF16)
    wct = jnp.concatenate([blocks(c_re), blocks(-c_im)], axis=-1).astype(BF16)
    return wb, wct


def kernel(x, c, w_ada, b_ada, w_in, lam_re, lam_im, log_dt, ssm_b_re, ssm_b_im, ssm_c_re, ssm_c_im, ssm_d, w_glu_val, w_glu_gate, w_pool, pool_scale, w_pool_out, w_out, ln1_g, ln1_b, w_ff1, w_ff2, ln2_g, ln2_b):
    bsz, seq, d = x.shape
    t = bsz * seq
    l = 0
    x2 = x.reshape(t, d)

    c_pad = jnp.zeros((SUBLANES, d), F32).at[:bsz].set(c)
    mod = _mod(c_pad, w_ada[l], b_ada[l][None, :])[:bsz]
    sh1, sc1, g1, sh2, sc2, g2 = [m[:, None, :] for m in jnp.split(mod, N_MOD, axis=-1)]

    proj = _in_proj(x2, sh1, sc1, w_in[l].astype(BF16), seq)

    a_re, a_im, al_re, al_im, bb_re, bb_im = _ssm_prep(
        lam_re[l], lam_im[l], log_dt[l], ssm_b_re[l], ssm_b_im[l])
    wb, wct = _block_diag_weights(bb_re, bb_im, ssm_c_re[l], ssm_c_im[l])
    a_all = jnp.stack([a_re, a_im, al_re, al_im])[:, ::SSM_GROUP]
    a_all = jnp.broadcast_to(a_all.reshape(4, SSM_GB, 1, SSM_ST),
                             (4, SSM_GB, SUBLANES, SSM_ST))

    later = [w_ff1[l], w_ff2[l], w_out[l], w_glu_val[l], w_glu_gate[l], w_pool_out[l]]
    z, w1_bf, w2_bf, wo_bf, wv_bf, wg_bf, wp_bf = _ssm(
        proj, wb, wct, a_all, ssm_d[l].reshape(SSM_GB, 1, SSM_CH), later, seq)

    yp = _pool(proj, w_pool[l].astype(BF16), pool_scale[l][None, :], seq)

    merged = _merge(z, yp, proj, wv_bf, wg_bf, wp_bf)

    x1, h2 = _out_ln(merged, wo_bf, x2, g1, ln1_g[l][None, :],
                     ln1_b[l][None, :], sh2, sc2, seq)

    out = _mlp(h2, w1_bf, w2_bf, x1, g2,
               ln2_g[l][None, :], ln2_b[l][None, :], seq)
    return out.reshape(bsz, seq, d)
```

```python
import functools
import math

import jax
import jax.numpy as jnp
from jax import lax
from jax.experimental import pallas as pl
from jax.experimental.pallas import tpu as pltpu

D_MODEL = 2048
SSM_WIDTH = 1024
SSM_GROUP = 16
SSM_GROUPS = 64
SSM_STATE = 64
POOL_WIDTH = 1024
POOL_WINDOWS = (2, 4, 8, 16)
POOL_GROUP = 256
IN_WIDTH = SSM_WIDTH + POOL_WIDTH + 2 * D_MODEL
D_FF = 4 * D_MODEL
N_MOD = 6
LN_EPS = 1e-5
ALPHA = 2.0 ** 0.25

SUBLANES = 8
LANES = 128
VMEM_LIMIT = 62 * 1024 * 1024

SSM_CHUNK = 128
SSM_ROWS = SUBLANES * SSM_CHUNK
SSM_GB = 4
SSM_CH = SSM_WIDTH // SSM_GB
SSM_ST = SSM_CH // SSM_GROUP * SSM_STATE
SSM_COLS = 512

BF16 = jnp.bfloat16
F32 = jnp.float32


def _cparams(sem):
    return pltpu.CompilerParams(dimension_semantics=sem, vmem_limit_bytes=VMEM_LIMIT)


def _ln(x):
    mu = jnp.mean(x, axis=-1, keepdims=True)
    xc = x - mu
    var = jnp.mean(xc * xc, axis=-1, keepdims=True)
    return xc * lax.rsqrt(var + LN_EPS)


def _mod_kernel(c_ref, w_ref, b_ref, o_ref):
    c = c_ref[...]
    c_act = (c * jax.nn.sigmoid(c)).astype(BF16)
    o_ref[...] = jnp.dot(c_act, w_ref[...].astype(BF16),
                         preferred_element_type=F32) + b_ref[...]


def _mod(c_pad, w_ada, b_ada):
    tn = 1024
    n = w_ada.shape[1]
    return pl.pallas_call(
        _mod_kernel,
        grid=(n // tn,),
        in_specs=[pl.BlockSpec((SUBLANES, D_MODEL), lambda j: (0, 0)),
                  pl.BlockSpec((D_MODEL, tn), lambda j: (0, j)),
                  pl.BlockSpec((1, tn), lambda j: (0, j))],
        out_specs=pl.BlockSpec((SUBLANES, tn), lambda j: (0, j)),
        out_shape=jax.ShapeDtypeStruct((SUBLANES, n), F32),
        compiler_params=_cparams(("arbitrary",)),
        name="mod",
    )(c_pad, w_ada, b_ada)


def _in_proj_kernel(x_ref, sh_ref, sc_ref, w_ref, proj_ref, h_ref):
    j = pl.program_id(1)
    half = x_ref.shape[0] // 2

    @pl.when(j == 0)
    def _():
        for s in range(2):
            rs = slice(s * half, (s + 1) * half)
            y = _ln(x_ref[rs, :])
            h_ref[rs, :] = (y * (1.0 + sc_ref[0]) + sh_ref[0]).astype(BF16)
            proj_ref[rs, :] = jnp.dot(h_ref[rs, :], w_ref[...],
                                      preferred_element_type=F32).astype(BF16)

    @pl.when(j != 0)
    def _():
        proj_ref[...] = jnp.dot(h_ref[...], w_ref[...],
                                preferred_element_type=F32).astype(BF16)


def _in_proj(x2, sh, sc, w_in, seq):
    tm, tn = 1024, 2048
    t = x2.shape[0]
    per_batch = seq // tm
    return pl.pallas_call(
        _in_proj_kernel,
        grid=(t // tm, IN_WIDTH // tn),
        in_specs=[pl.BlockSpec((tm, D_MODEL), lambda i, j: (i, 0)),
                  pl.BlockSpec((1, 1, D_MODEL), lambda i, j: (i // per_batch, 0, 0)),
                  pl.BlockSpec((1, 1, D_MODEL), lambda i, j: (i // per_batch, 0, 0)),
                  pl.BlockSpec((D_MODEL, tn), lambda i, j: (0, j))],
        out_specs=pl.BlockSpec((tm, tn), lambda i, j: (i, j)),
        out_shape=jax.ShapeDtypeStruct((t, IN_WIDTH), BF16),
        scratch_shapes=[pltpu.VMEM((tm, D_MODEL), BF16)],
        compiler_params=_cparams(("parallel", "arbitrary")),
        name="in_proj",
    )(x2, sh, sc, w_in)


def _zoh_pole(lr, li, dt, steps):
    mag = jnp.exp(lr * dt * steps)
    ang = li * dt * steps
    return mag * jnp.cos(ang), mag * jnp.sin(ang)


def _ssm_prep_kernel(lr_ref, li_ref, ldt_ref, br_ref, bi_ref, cr_ref, ci_ref,
                     lrs_ref, lis_ref, ldts_ref, wb_ref, wct_ref, a_ref):
    lr = lr_ref[...]
    li = li_ref[...]
    ab_re, ab_im = _zoh_pole(lr, li, jnp.exp(ldt_ref[...]), 1.0)
    num_re = ab_re - 1.0
    num_im = ab_im
    den = lr * lr + li * li
    f_re = (num_re * lr + num_im * li) / den
    f_im = (num_im * lr - num_re * li) / den
    br = br_ref[...]
    bi = bi_ref[...]
    bb_re = f_re * br - f_im * bi
    bb_im = f_re * bi + f_im * br

    state = lax.broadcasted_iota(jnp.int32, (SSM_STATE, SSM_ST), 0)
    lane = lax.broadcasted_iota(jnp.int32, (SSM_STATE, SSM_ST), 1)
    expand = ((lane & (SSM_STATE - 1)) == state).astype(BF16)
    row_g = lax.broadcasted_iota(jnp.int32, (SSM_CH, SSM_ST), 0) >> (SSM_GROUP.bit_length() - 1)
    col_g = lax.broadcasted_iota(jnp.int32, (SSM_CH, SSM_ST), 1) >> (SSM_STATE.bit_length() - 1)
    on_diag = row_g == col_g

    def place(w):
        t = jnp.dot(w.astype(BF16), expand, preferred_element_type=F32)
        return jnp.where(on_diag, t, 0.0).astype(BF16)

    wb_ref[0, :, 0:SSM_ST] = place(bb_re)
    wb_ref[0, :, SSM_ST:2 * SSM_ST] = place(bb_im)
    wct_ref[0, :, 0:SSM_ST] = place(cr_ref[...])
    wct_ref[0, :, SSM_ST:2 * SSM_ST] = place(-ci_ref[...])

    shape = (SUBLANES, SSM_ST)
    lrs = jnp.broadcast_to(lrs_ref[0], shape)
    lis = jnp.broadcast_to(lis_ref[0], shape)
    dts = jnp.exp(jnp.broadcast_to(ldts_ref[0], shape))
    a_ref[0, 0], a_ref[1, 0] = _zoh_pole(lrs, lis, dts, 1.0)
    a_ref[2, 0], a_ref[3, 0] = _zoh_pole(lrs, lis, dts, float(SSM_CHUNK))


def _ssm_prep(lam_re, lam_im, log_dt, b_re, b_im, c_re, c_im):
    rep = lambda a: jnp.repeat(a, SSM_GROUP, axis=0)
    ldt = jnp.broadcast_to(log_dt[:, None], (SSM_GROUPS, SSM_STATE))
    rows = [rep(lam_re), rep(lam_im), rep(ldt),
            b_re.transpose(0, 2, 1).reshape(SSM_WIDTH, SSM_STATE),
            b_im.transpose(0, 2, 1).reshape(SSM_WIDTH, SSM_STATE),
            c_re.reshape(SSM_WIDTH, SSM_STATE),
            c_im.reshape(SSM_WIDTH, SSM_STATE)]
    lanes = [a.reshape(SSM_GB, 1, SSM_ST) for a in (lam_re, lam_im, ldt)]
    row_spec = pl.BlockSpec((SSM_CH, SSM_STATE), lambda g: (g, 0))
    lane_spec = pl.BlockSpec((1, 1, SSM_ST), lambda g: (g, 0, 0))
    w_spec = pl.BlockSpec((1, SSM_CH, 2 * SSM_ST), lambda g: (g, 0, 0))
    w_shape = jax.ShapeDtypeStruct((SSM_GB, SSM_CH, 2 * SSM_ST), BF16)
    return pl.pallas_call(
        _ssm_prep_kernel,
        grid=(SSM_GB,),
        in_specs=[row_spec] * len(rows) + [lane_spec] * len(lanes),
        out_specs=[w_spec, w_spec,
                   pl.BlockSpec((4, 1, SUBLANES, SSM_ST), lambda g: (0, g, 0, 0))],
        out_shape=[w_shape, w_shape,
                   jax.ShapeDtypeStruct((4, SSM_GB, SUBLANES, SSM_ST), F32)],
        compiler_params=_cparams(("parallel",)),
        name="ssm_prep",
    )(*rows, *lanes)


def _sublane_transpose(blocks):
    sub = lax.broadcasted_iota(jnp.int32, blocks[0].shape, 1)
    a = list(blocks)
    for s in (1, 2, 4):
        keep = (sub & s) == 0
        nxt = list(a)
        for r in range(SUBLANES):
            if r & s == 0:
                lo, hi = a[r], a[r + s]
                nxt[r] = jnp.where(keep, lo, pltpu.roll(hi, s, 1))
                nxt[r + s] = jnp.where(keep, pltpu.roll(lo, SUBLANES - s, 1), hi)
        a = nxt
    return a


def _ssm_kernel(u_ref, wb_ref, wct_ref, a_ref, d_ref, *refs, n_cast):
    cast_in = refs[:n_cast]
    z_ref = refs[n_cast]
    cast_out = refs[n_cast + 1:2 * n_cast + 1]
    up_ref, upb_ref, bu_ref, xb_ref, cst_ref, st_ref, zp_ref = refs[2 * n_cast + 1:]
    k = pl.program_id(2)
    L = SSM_CHUNK
    lh = L // SUBLANES

    for src, dst in zip(cast_in, cast_out):
        dst[...] = src[...].astype(BF16)

    @pl.when(k == 0)
    def _():
        cst_ref[...] = jnp.zeros_like(cst_ref)

    u4 = u_ref[...].astype(F32).reshape(SUBLANES, lh, SUBLANES, SSM_CH)
    for l_lo, blk in enumerate(_sublane_transpose([u4[r] for r in range(SUBLANES)])):
        up_ref[:, l_lo] = blk
    upb_ref[...] = up_ref[...].reshape(SSM_ROWS, SSM_CH).astype(BF16)

    for n in range(2 * SSM_ST // 256):
        bu_ref[:, n * 256:(n + 1) * 256] = jnp.dot(
            upb_ref[...], wb_ref[0, :, n * 256:(n + 1) * 256],
            preferred_element_type=F32)

    row_id = lax.broadcasted_iota(jnp.int32, (SUBLANES, SSM_COLS), 0)
    for cg in range(SSM_ST // SSM_COLS):
        re_sl = slice(cg * SSM_COLS, (cg + 1) * SSM_COLS)
        im_sl = slice(SSM_ST + cg * SSM_COLS, SSM_ST + (cg + 1) * SSM_COLS)
        ar = a_ref[0, 0, :, re_sl]
        ai = a_ref[1, 0, :, re_sl]

        def step(l, sr, si):
            row = pl.multiple_of(l * SUBLANES, SUBLANES)
            br = bu_ref[pl.ds(row, SUBLANES), re_sl]
            bi = bu_ref[pl.ds(row, SUBLANES), im_sl]
            return ar * sr - ai * si + br, ar * si + ai * sr + bi

        def pass1(l, carry):
            return step(l, *carry)
        er, ei = lax.fori_loop(0, L, pass1, (cst_ref[:, re_sl], cst_ref[:, im_sl]),
                               unroll=2)

        alr = a_ref[2, 0, :, re_sl]
        ali = a_ref[3, 0, :, re_sl]
        tr, ti = er, ei
        for r in range(1, SUBLANES):
            pr = pltpu.roll(tr, 1, 0)
            pi = pltpu.roll(ti, 1, 0)
            tr = jnp.where(row_id == r, er + alr * pr - ali * pi, tr)
            ti = jnp.where(row_id == r, ei + alr * pi + ali * pr, ti)
        pr = pltpu.roll(tr, 1, 0)
        pi = pltpu.roll(ti, 1, 0)
        first = row_id == 0
        st_ref[:, re_sl] = jnp.where(first, cst_ref[:, re_sl], pr)
        st_ref[:, im_sl] = jnp.where(first, cst_ref[:, im_sl], pi)
        cst_ref[:, re_sl] = jnp.where(first, pr, 0.0)
        cst_ref[:, im_sl] = jnp.where(first, pi, 0.0)

        def pass2(m, carry):
            sr, si = carry
            r1, i1 = step(2 * m, sr, si)
            r2, i2 = step(2 * m + 1, r1, i1)
            row = pl.multiple_of(m * 2 * SUBLANES, 2 * SUBLANES)
            xb_ref[pl.ds(row, 2 * SUBLANES), re_sl] = (
                jnp.concatenate([r1, r2], axis=0).astype(BF16))
            xb_ref[pl.ds(row, 2 * SUBLANES), im_sl] = (
                jnp.concatenate([i1, i2], axis=0).astype(BF16))
            return r2, i2
        lax.fori_loop(0, L // 2, pass2, (st_ref[:, re_sl], st_ref[:, im_sl]),
                      unroll=2)

    mrows = 256
    mh = mrows // (SUBLANES * SUBLANES)
    for m in range(SSM_ROWS // mrows):
        rs = slice(m * mrows, (m + 1) * mrows)
        y = lax.dot_general(xb_ref[rs, :], wct_ref[0], (((1,), (1,)), ((), ())),
                            preferred_element_type=F32)
        y = y + d_ref[0] * up_ref[m * mh:(m + 1) * mh].reshape(mrows, SSM_CH)
        zp_ref[m * mh:(m + 1) * mh] = jax.nn.gelu(y, approximate=True).reshape(
            mh, SUBLANES, SUBLANES, SSM_CH)

    blocks = _sublane_transpose([zp_ref[:, l_lo] for l_lo in range(SUBLANES)])
    for r, blk in enumerate(blocks):
        z_ref[r * L:(r + 1) * L, :] = blk.reshape(L, SSM_CH).astype(BF16)


def _ssm(proj, wb, wct, a_all, d, weights, seq):
    t = proj.shape[0]
    nb = t // seq
    nk = seq // SSM_ROWS
    steps = SSM_GB * nb * nk
    tok = lambda g, b, k: (b * nk + k, g)
    par = lambda g, b, k: (g, 0, 0)
    step = lambda g, b, k: ((g * nb + b) * nk + k, 0)
    lh = SSM_CHUNK // SUBLANES
    cast_specs = [pl.BlockSpec((w.shape[0] // steps, w.shape[1]), step) for w in weights]
    return pl.pallas_call(
        functools.partial(_ssm_kernel, n_cast=len(weights)),
        grid=(SSM_GB, nb, nk),
        in_specs=[pl.BlockSpec((SSM_ROWS, SSM_CH), tok),
                  pl.BlockSpec((1, SSM_CH, 2 * SSM_ST), par),
                  pl.BlockSpec((1, SSM_CH, 2 * SSM_ST), par),
                  pl.BlockSpec((4, 1, SUBLANES, SSM_ST), lambda g, b, k: (0, g, 0, 0)),
                  pl.BlockSpec((1, 1, SSM_CH), par)] + cast_specs,
        out_specs=[pl.BlockSpec((SSM_ROWS, SSM_CH), tok)] + cast_specs,
        out_shape=[jax.ShapeDtypeStruct((t, SSM_WIDTH), BF16)]
        + [jax.ShapeDtypeStruct(w.shape, BF16) for w in weights],
        scratch_shapes=[pltpu.VMEM((lh, SUBLANES, SUBLANES, SSM_CH), F32),
                        pltpu.VMEM((SSM_ROWS, SSM_CH), BF16),
                        pltpu.VMEM((SSM_ROWS, 2 * SSM_ST), F32),
                        pltpu.VMEM((SSM_ROWS, 2 * SSM_ST), BF16),
                        pltpu.VMEM((SUBLANES, 2 * SSM_ST), F32),
                        pltpu.VMEM((SUBLANES, 2 * SSM_ST), F32),
                        pltpu.VMEM((lh, SUBLANES, SUBLANES, SSM_CH), F32)],
        compiler_params=_cparams(("arbitrary", "arbitrary", "arbitrary")),
        name="ssm",
    )(proj, wb, wct, a_all, d, *weights)


POOL_HALO = 32
POOL_LEAD = 16


def _pool_kernel(u_ref, w_ref, s_ref, o_ref, ext_ref, sum_ref, *, tt):
    k = pl.program_id(1)
    base = POOL_HALO - POOL_LEAD
    n = tt + POOL_LEAD

    @pl.when(k == 0)
    def _():
        ext_ref[0:POOL_HALO, :] = jnp.zeros((POOL_HALO, POOL_WIDTH), F32)

    @pl.when(k != 0)
    def _():
        ext_ref[0:POOL_HALO, :] = ext_ref[tt:tt + POOL_HALO, :]

    ext_ref[POOL_HALO:POOL_HALO + tt, :] = u_ref[...].astype(F32)
    sum_ref[0:base, :] = jnp.zeros((base, POOL_GROUP), F32)
    pos = (k * tt + lax.broadcasted_iota(jnp.int32, (tt, POOL_GROUP), 0)).astype(F32)
    for gi, w in enumerate(POOL_WINDOWS):
        cs = slice(gi * POOL_GROUP, (gi + 1) * POOL_GROUP)
        tot = ext_ref[base:base + n, cs] + ext_ref[base - 1:base - 1 + n, cs]
        m = 2
        while m < w:
            sum_ref[base:base + n, :] = tot
            tot = tot + sum_ref[base - m:base - m + n, :]
            m *= 2
        cur = ext_ref[POOL_HALO:POOL_HALO + tt, cs]
        count = jnp.minimum(pos + 1.0, float(w))
        pooled = tot[POOL_LEAD:, :] / count - cur
        y = jnp.dot(pooled.astype(BF16), w_ref[gi], preferred_element_type=F32)
        o_ref[:, cs] = (y * s_ref[:, cs]).astype(BF16)


def _pool(proj, w_pool, pool_scale, seq):
    tt = 512
    t = proj.shape[0]
    nk = seq // tt
    return pl.pallas_call(
        functools.partial(_pool_kernel, tt=tt),
        grid=(t // seq, nk),
        in_specs=[pl.BlockSpec((tt, POOL_WIDTH), lambda b, k: (b * nk + k, 1)),
                  pl.BlockSpec((len(POOL_WINDOWS), POOL_GROUP, POOL_GROUP),
                               lambda b, k: (0, 0, 0)),
                  pl.BlockSpec((1, POOL_WIDTH), lambda b, k: (0, 0))],
        out_specs=pl.BlockSpec((tt, POOL_WIDTH), lambda b, k: (b * nk + k, 0)),
        out_shape=jax.ShapeDtypeStruct((t, POOL_WIDTH), BF16),
        scratch_shapes=[pltpu.VMEM((tt + POOL_HALO, POOL_WIDTH), F32),
                        pltpu.VMEM((tt + POOL_HALO, POOL_GROUP), F32)],
        compiler_params=_cparams(("parallel", "arbitrary")),
        name="pool",
    )(proj, w_pool, pool_scale)


def _merge_kernel(z_ref, yp_ref, ga_ref, gb_ref, wv_ref, wg_ref, wp_ref, o_ref):
    half = z_ref.shape[0] // 2
    for s in range(2):
        rs = slice(s * half, (s + 1) * half)
        z = z_ref[rs, :]
        val = jnp.dot(z, wv_ref[...], preferred_element_type=F32)
        gate = jnp.dot(z, wg_ref[...], preferred_element_type=F32)
        y_a = val * jax.nn.sigmoid(gate)
        y_b = jnp.dot(yp_ref[rs, :], wp_ref[...], preferred_element_type=F32)
        merged = (jax.nn.sigmoid(ga_ref[rs, :].astype(F32)) * y_a
                  + jax.nn.sigmoid(gb_ref[rs, :].astype(F32)) * y_b)
        o_ref[rs, :] = merged.astype(BF16)


def _merge(z, yp, proj, w_val, w_gate, w_po):
    tm, tn = 1024, 1024
    t = z.shape[0]
    ga0 = (SSM_WIDTH + POOL_WIDTH) // tn
    gb0 = ga0 + D_MODEL // tn
    wspec = pl.BlockSpec((SSM_WIDTH, tn), lambda i, j: (0, j))
    return pl.pallas_call(
        _merge_kernel,
        grid=(t // tm, D_MODEL // tn),
        in_specs=[pl.BlockSpec((tm, SSM_WIDTH), lambda i, j: (i, 0)),
                  pl.BlockSpec((tm, POOL_WIDTH), lambda i, j: (i, 0)),
                  pl.BlockSpec((tm, tn), lambda i, j: (i, ga0 + j)),
                  pl.BlockSpec((tm, tn), lambda i, j: (i, gb0 + j)),
                  wspec, wspec, wspec],
        out_specs=pl.BlockSpec((tm, tn), lambda i, j: (i, j)),
        out_shape=jax.ShapeDtypeStruct((t, D_MODEL), BF16),
        compiler_params=_cparams(("parallel", "arbitrary")),
        name="merge",
    )(z, yp, proj, proj, w_val, w_gate, w_po)


def _out_ln_kernel(m_ref, w_ref, x_ref, g1_ref, lg_ref, lb_ref, sh_ref, sc_ref,
                   x1_ref, h2_ref):
    half = m_ref.shape[0] // 2
    for s in range(2):
        rs = slice(s * half, (s + 1) * half)
        y = jnp.dot(m_ref[rs, :], w_ref[...], preferred_element_type=F32)
        r = ALPHA * x_ref[rs, :] + g1_ref[0] * y
        x1 = _ln(r) * lg_ref[...] + lb_ref[...]
        x1_ref[rs, :] = x1
        h2_ref[rs, :] = (_ln(x1) * (1.0 + sc_ref[0]) + sh_ref[0]).astype(BF16)


def _out_ln(merged, w_out, x2, g1, ln_g, ln_b, sh2, sc2, seq):
    tm = 512
    t = x2.shape[0]
    per_batch = seq // tm
    row = pl.BlockSpec((tm, D_MODEL), lambda i: (i, 0))
    vec = pl.BlockSpec((1, D_MODEL), lambda i: (0, 0))
    bvec = pl.BlockSpec((1, 1, D_MODEL), lambda i: (i // per_batch, 0, 0))
    return pl.pallas_call(
        _out_ln_kernel,
        grid=(t // tm,),
        in_specs=[row,
                  pl.BlockSpec((D_MODEL, D_MODEL), lambda i: (0, 0),
                               pipeline_mode=pl.Buffered(1)),
                  row, bvec, vec, vec, bvec, bvec],
        out_specs=[row, row],
        out_shape=[jax.ShapeDtypeStruct((t, D_MODEL), F32),
                   jax.ShapeDtypeStruct((t, D_MODEL), BF16)],
        compiler_params=_cparams(("parallel",)),
        name="out_ln",
    )(merged, w_out, x2, g1, ln_g, ln_b, sh2, sc2)


def _mlp_kernel(h_ref, w1_ref, w2_ref, x1_ref, g2_ref, lg_ref, lb_ref, o_ref):
    f = pl.program_id(1)
    last = pl.num_programs(1) - 1
    half = h_ref.shape[0] // 2

    def ff(rows):
        a = jnp.maximum(jnp.dot(h_ref[rows, :], w1_ref[...],
                                preferred_element_type=F32), 0.0)
        return jnp.dot((a * a).astype(BF16), w2_ref[...], preferred_element_type=F32)

    @pl.when(f == 0)
    def _():
        o_ref[...] = ff(slice(None))

    @pl.when(jnp.logical_and(f > 0, f < last))
    def _():
        o_ref[...] += ff(slice(None))

    @pl.when(f == last)
    def _():
        for s in range(2):
            rs = slice(s * half, (s + 1) * half)
            r = ALPHA * x1_ref[rs, :] + g2_ref[0] * (o_ref[rs, :] + ff(rs))
            o_ref[rs, :] = _ln(r) * lg_ref[...] + lb_ref[...]


def _mlp(h2, w1, w2, x1, g2, ln_g, ln_b, seq):
    tm, tf = 1024, 1024
    t = h2.shape[0]
    per_batch = seq // tm
    once = pl.Buffered(1)
    vec = pl.BlockSpec((1, D_MODEL), lambda i, f: (0, 0))
    return pl.pallas_call(
        _mlp_kernel,
        grid=(t // tm, D_FF // tf),
        in_specs=[pl.BlockSpec((tm, D_MODEL), lambda i, f: (i, 0), pipeline_mode=once),
                  pl.BlockSpec((D_MODEL, tf), lambda i, f: (0, f)),
                  pl.BlockSpec((tf, D_MODEL), lambda i, f: (f, 0)),
                  pl.BlockSpec((tm, D_MODEL), lambda i, f: (i, 0), pipeline_mode=once),
                  pl.BlockSpec((1, 1, D_MODEL), lambda i, f: (i // per_batch, 0, 0)),
                  vec, vec],
        out_specs=pl.BlockSpec((tm, D_MODEL), lambda i, f: (i, 0)),
        out_shape=jax.ShapeDtypeStruct((t, D_MODEL), F32),
        compiler_params=_cparams(("parallel", "arbitrary")),
        name="mlp",
    )(h2, w1, w2, x1, g2, ln_g, ln_b)


def kernel(x, c, w_ada, b_ada, w_in, lam_re, lam_im, log_dt, ssm_b_re, ssm_b_im, ssm_c_re, ssm_c_im, ssm_d, w_glu_val, w_glu_gate, w_pool, pool_scale, w_pool_out, w_out, ln1_g, ln1_b, w_ff1, w_ff2, ln2_g, ln2_b):
    bsz, seq, d = x.shape
    t = bsz * seq
    l = 0
    x2 = x.reshape(t, d)

    c_pad = jnp.zeros((SUBLANES, d), F32).at[:bsz].set(c)
    mod = _mod(c_pad, w_ada[l], b_ada[l][None, :])[:bsz]
    sh1, sc1, g1, sh2, sc2, g2 = [m[:, None, :] for m in jnp.split(mod, N_MOD, axis=-1)]

    proj = _in_proj(x2, sh1, sc1, w_in[l].astype(BF16), seq)

    wb, wct, a_all = _ssm_prep(lam_re[l], lam_im[l], log_dt[l], ssm_b_re[l], ssm_b_im[l],
                               ssm_c_re[l], ssm_c_im[l])
    later = [w_ff1[l], w_ff2[l], w_out[l], w_glu_val[l], w_glu_gate[l], w_pool_out[l]]
    z, w1_bf, w2_bf, wo_bf, wv_bf, wg_bf, wp_bf = _ssm(
        proj, wb, wct, a_all, ssm_d[l].reshape(SSM_GB, 1, SSM_CH), later, seq)

    yp = _pool(proj, w_pool[l].astype(BF16), pool_scale[l][None, :], seq)

    merged = _merge(z, yp, proj, wv_bf, wg_bf, wp_bf)

    x1, h2 = _out_ln(merged, wo_bf, x2, g1, ln1_g[l][None, :],
                     ln1_b[l][None, :], sh2, sc2, seq)

    out = _mlp(h2, w1_bf, w2_bf, x1, g2,
               ln2_g[l][None, :], ln2_b[l][None, :], seq)
    return out.reshape(bsz, seq, d)
```

```python
import functools
import math

import jax
import jax.numpy as jnp
from jax import lax
from jax.experimental import pallas as pl
from jax.experimental.pallas import tpu as pltpu

D_MODEL = 2048
SSM_WIDTH = 1024
SSM_GROUP = 16
SSM_GROUPS = 64
SSM_STATE = 64
POOL_WIDTH = 1024
POOL_WINDOWS = (2, 4, 8, 16)
POOL_GROUP = 256
IN_WIDTH = SSM_WIDTH + POOL_WIDTH + 2 * D_MODEL
D_FF = 4 * D_MODEL
N_MOD = 6
MOD_SH1, MOD_SC1, MOD_G1, MOD_SH2, MOD_SC2, MOD_G2 = range(N_MOD)
LN_EPS = 1e-5
ALPHA = 2.0 ** 0.25

SUBLANES = 8
LANES = 128
VMEM_LIMIT = 62 * 1024 * 1024

SSM_CHUNK = 128
SSM_ROWS = SUBLANES * SSM_CHUNK
SSM_GB = 4
SSM_CH = SSM_WIDTH // SSM_GB
SSM_ST = SSM_CH // SSM_GROUP * SSM_STATE
SSM_COLS = 512

BF16 = jnp.bfloat16
F32 = jnp.float32


def _cparams(sem):
    return pltpu.CompilerParams(dimension_semantics=sem, vmem_limit_bytes=VMEM_LIMIT)


def _ln(x):
    mu = jnp.mean(x, axis=-1, keepdims=True)
    xc = x - mu
    var = jnp.mean(xc * xc, axis=-1, keepdims=True)
    return xc * lax.rsqrt(var + LN_EPS)


def _mod_kernel(c_ref, w_ref, b_ref, o_ref):
    c = c_ref[...]
    c_act = (c * jax.nn.sigmoid(c)).astype(BF16)
    o_ref[...] = jnp.dot(c_act, w_ref[...].astype(BF16),
                         preferred_element_type=F32) + b_ref[...]


def _mod(c_pad, w_ada, b_ada):
    tn = 2048
    n = w_ada.shape[1]
    return pl.pallas_call(
        _mod_kernel,
        grid=(n // tn,),
        in_specs=[pl.BlockSpec((SUBLANES, D_MODEL), lambda j: (0, 0)),
                  pl.BlockSpec((D_MODEL, tn), lambda j: (0, j)),
                  pl.BlockSpec((1, tn), lambda j: (0, j))],
        out_specs=pl.BlockSpec((SUBLANES, tn), lambda j: (0, j)),
        out_shape=jax.ShapeDtypeStruct((SUBLANES, n), F32),
        compiler_params=_cparams(("arbitrary",)),
        name="mod",
    )(c_pad, w_ada, b_ada)


def _in_proj_kernel(x_ref, sh_ref, sc_ref, w_ref, proj_ref, h_ref):
    j = pl.program_id(1)
    half = x_ref.shape[0] // 2

    @pl.when(j == 0)
    def _():
        for s in range(2):
            rs = slice(s * half, (s + 1) * half)
            y = _ln(x_ref[rs, :])
            h_ref[rs, :] = (y * (1.0 + sc_ref[0]) + sh_ref[0]).astype(BF16)
            proj_ref[rs, :] = jnp.dot(h_ref[rs, :], w_ref[...],
                                      preferred_element_type=F32).astype(BF16)

    @pl.when(j != 0)
    def _():
        proj_ref[...] = jnp.dot(h_ref[...], w_ref[...],
                                preferred_element_type=F32).astype(BF16)


def _in_proj(x2, mod3, w_in, seq):
    tm, tn = 1024, 2048
    t = x2.shape[0]
    per_batch = seq // tm
    return pl.pallas_call(
        _in_proj_kernel,
        grid=(t // tm, IN_WIDTH // tn),
        in_specs=[pl.BlockSpec((tm, D_MODEL), lambda i, j: (i, 0)),
                  pl.BlockSpec((1, 1, D_MODEL), lambda i, j: (i // per_batch, 0, MOD_SH1)),
                  pl.BlockSpec((1, 1, D_MODEL), lambda i, j: (i // per_batch, 0, MOD_SC1)),
                  pl.BlockSpec((D_MODEL, tn), lambda i, j: (0, j))],
        out_specs=pl.BlockSpec((tm, tn), lambda i, j: (i, j)),
        out_shape=jax.ShapeDtypeStruct((t, IN_WIDTH), BF16),
        scratch_shapes=[pltpu.VMEM((tm, D_MODEL), BF16)],
        compiler_params=_cparams(("parallel", "arbitrary")),
        name="in_proj",
    )(x2, mod3, mod3, w_in)


def _zoh_pole(lr, li, dt, steps):
    mag = jnp.exp(lr * dt * steps)
    ang = li * dt * steps
    return mag * jnp.cos(ang), mag * jnp.sin(ang)


def _ssm_prep_kernel(lr_ref, li_ref, ldt_ref, br_ref, bi_ref, cr_ref, ci_ref,
                     lrs_ref, lis_ref, ldts_ref, wb_ref, wct_ref, a_ref):
    lr = lr_ref[...]
    li = li_ref[...]
    ab_re, ab_im = _zoh_pole(lr, li, jnp.exp(ldt_ref[...]), 1.0)
    num_re = ab_re - 1.0
    num_im = ab_im
    den = lr * lr + li * li
    f_re = (num_re * lr + num_im * li) / den
    f_im = (num_im * lr - num_re * li) / den
    br = br_ref[...]
    bi = bi_ref[...]
    bb_re = f_re * br - f_im * bi
    bb_im = f_re * bi + f_im * br

    state = lax.broadcasted_iota(jnp.int32, (SSM_STATE, SSM_ST), 0)
    lane = lax.broadcasted_iota(jnp.int32, (SSM_STATE, SSM_ST), 1)
    expand = ((lane & (SSM_STATE - 1)) == state).astype(BF16)
    row_g = lax.broadcasted_iota(jnp.int32, (SSM_CH, SSM_ST), 0) >> (SSM_GROUP.bit_length() - 1)
    col_g = lax.broadcasted_iota(jnp.int32, (SSM_CH, SSM_ST), 1) >> (SSM_STATE.bit_length() - 1)
    on_diag = row_g == col_g

    def place(w):
        t = jnp.dot(w.astype(BF16), expand, preferred_element_type=F32)
        return jnp.where(on_diag, t, 0.0).astype(BF16)

    wb_ref[0, :, 0:SSM_ST] = place(bb_re)
    wb_ref[0, :, SSM_ST:2 * SSM_ST] = place(bb_im)
    wct_ref[0, :, 0:SSM_ST] = place(cr_ref[...])
    wct_ref[0, :, SSM_ST:2 * SSM_ST] = place(-ci_ref[...])

    shape = (SUBLANES, SSM_ST)
    lrs = jnp.broadcast_to(lrs_ref[0], shape)
    lis = jnp.broadcast_to(lis_ref[0], shape)
    dts = jnp.exp(jnp.broadcast_to(ldts_ref[0], shape))
    a_ref[0, 0], a_ref[1, 0] = _zoh_pole(lrs, lis, dts, 1.0)
    a_ref[2, 0], a_ref[3, 0] = _zoh_pole(lrs, lis, dts, float(SSM_CHUNK))


def _ssm_prep(lam_re, lam_im, log_dt, b_re, b_im, c_re, c_im):
    rep = lambda a: jnp.repeat(a, SSM_GROUP, axis=0)
    ldt = jnp.broadcast_to(log_dt[:, None], (SSM_GROUPS, SSM_STATE))
    rows = [rep(lam_re), rep(lam_im), rep(ldt),
            b_re.transpose(0, 2, 1).reshape(SSM_WIDTH, SSM_STATE),
            b_im.transpose(0, 2, 1).reshape(SSM_WIDTH, SSM_STATE),
            c_re.reshape(SSM_WIDTH, SSM_STATE),
            c_im.reshape(SSM_WIDTH, SSM_STATE)]
    lanes = [a.reshape(SSM_GB, 1, SSM_ST) for a in (lam_re, lam_im, ldt)]
    row_spec = pl.BlockSpec((SSM_CH, SSM_STATE), lambda g: (g, 0))
    lane_spec = pl.BlockSpec((1, 1, SSM_ST), lambda g: (g, 0, 0))
    w_spec = pl.BlockSpec((1, SSM_CH, 2 * SSM_ST), lambda g: (g, 0, 0))
    w_shape = jax.ShapeDtypeStruct((SSM_GB, SSM_CH, 2 * SSM_ST), BF16)
    return pl.pallas_call(
        _ssm_prep_kernel,
        grid=(SSM_GB,),
        in_specs=[row_spec] * len(rows) + [lane_spec] * len(lanes),
        out_specs=[w_spec, w_spec,
                   pl.BlockSpec((4, 1, SUBLANES, SSM_ST), lambda g: (0, g, 0, 0))],
        out_shape=[w_shape, w_shape,
                   jax.ShapeDtypeStruct((4, SSM_GB, SUBLANES, SSM_ST), F32)],
        compiler_params=_cparams(("parallel",)),
        name="ssm_prep",
    )(*rows, *lanes)


def _sublane_transpose(blocks):
    sub = lax.broadcasted_iota(jnp.int32, blocks[0].shape, 1)
    a = list(blocks)
    for s in (1, 2, 4):
        keep = (sub & s) == 0
        nxt = list(a)
        for r in range(SUBLANES):
            if r & s == 0:
                lo, hi = a[r], a[r + s]
                nxt[r] = jnp.where(keep, lo, pltpu.roll(hi, s, 1))
                nxt[r + s] = jnp.where(keep, pltpu.roll(lo, SUBLANES - s, 1), hi)
        a = nxt
    return a


def _ssm_kernel(u_ref, wb_ref, wct_ref, a_ref, d_ref, *refs, n_cast):
    cast_in = refs[:n_cast]
    z_ref = refs[n_cast]
    cast_out = refs[n_cast + 1:2 * n_cast + 1]
    up_ref, upb_ref, bu_ref, xb_ref, cst_ref, st_ref = refs[2 * n_cast + 1:]
    k = pl.program_id(2)
    L = SSM_CHUNK
    lh = L // SUBLANES

    @pl.when(k == 0)
    def _():
        cst_ref[...] = jnp.zeros_like(cst_ref)

    u4 = u_ref[...].astype(F32).reshape(SUBLANES, lh, SUBLANES, SSM_CH)
    for l_lo, blk in enumerate(_sublane_transpose([u4[r] for r in range(SUBLANES)])):
        up_ref[:, l_lo] = blk
    upb_ref[...] = up_ref[...].reshape(SSM_ROWS, SSM_CH).astype(BF16)

    casts = list(zip(cast_in, cast_out))
    n_tiles = 2 * SSM_ST // 256
    assert len(casts) <= n_tiles
    for n in range(n_tiles):
        bu_ref[:, n * 256:(n + 1) * 256] = jnp.dot(
            upb_ref[...], wb_ref[0, :, n * 256:(n + 1) * 256],
            preferred_element_type=F32)
        if n < len(casts):
            src, dst = casts[n]
            dst[...] = src[...].astype(BF16)

    row_id = lax.broadcasted_iota(jnp.int32, (SUBLANES, SSM_COLS), 0)
    for cg in range(SSM_ST // SSM_COLS):
        re_sl = slice(cg * SSM_COLS, (cg + 1) * SSM_COLS)
        im_sl = slice(SSM_ST + cg * SSM_COLS, SSM_ST + (cg + 1) * SSM_COLS)
        ar = a_ref[0, 0, :, re_sl]
        ai = a_ref[1, 0, :, re_sl]

        def step(l, sr, si):
            row = pl.multiple_of(l * SUBLANES, SUBLANES)
            br = bu_ref[pl.ds(row, SUBLANES), re_sl]
            bi = bu_ref[pl.ds(row, SUBLANES), im_sl]
            return ar * sr - ai * si + br, ar * si + ai * sr + bi

        def pass1(l, carry):
            return step(l, *carry)
        er, ei = lax.fori_loop(0, L, pass1, (cst_ref[:, re_sl], cst_ref[:, im_sl]),
                               unroll=2)

        alr = a_ref[2, 0, :, re_sl]
        ali = a_ref[3, 0, :, re_sl]
        tr, ti = er, ei
        for r in range(1, SUBLANES):
            pr = pltpu.roll(tr, 1, 0)
            pi = pltpu.roll(ti, 1, 0)
            tr = jnp.where(row_id == r, er + alr * pr - ali * pi, tr)
            ti = jnp.where(row_id == r, ei + alr * pi + ali * pr, ti)
        pr = pltpu.roll(tr, 1, 0)
        pi = pltpu.roll(ti, 1, 0)
        first = row_id == 0
        st_ref[:, re_sl] = jnp.where(first, cst_ref[:, re_sl], pr)
        st_ref[:, im_sl] = jnp.where(first, cst_ref[:, im_sl], pi)
        cst_ref[:, re_sl] = jnp.where(first, pr, 0.0)
        cst_ref[:, im_sl] = jnp.where(first, pi, 0.0)

        def pass2(m, carry):
            sr, si = carry
            r1, i1 = step(2 * m, sr, si)
            r2, i2 = step(2 * m + 1, r1, i1)
            row = pl.multiple_of(m * 2 * SUBLANES, 2 * SUBLANES)
            xb_ref[pl.ds(row, 2 * SUBLANES), re_sl] = (
                jnp.concatenate([r1, r2], axis=0).astype(BF16))
            xb_ref[pl.ds(row, 2 * SUBLANES), im_sl] = (
                jnp.concatenate([i1, i2], axis=0).astype(BF16))
            return r2, i2
        lax.fori_loop(0, L // 2, pass2, (st_ref[:, re_sl], st_ref[:, im_sl]),
                      unroll=2)

    mrows = 256
    mh = mrows // (SUBLANES * SUBLANES)
    for m in range(SSM_ROWS // mrows):
        rs = slice(m * mrows, (m + 1) * mrows)
        y = lax.dot_general(xb_ref[rs, :], wct_ref[0], (((1,), (1,)), ((), ())),
                            preferred_element_type=F32)
        y = y + d_ref[0] * up_ref[m * mh:(m + 1) * mh].reshape(mrows, SSM_CH)
        g4 = jax.nn.gelu(y, approximate=True).reshape(mh, SUBLANES, SUBLANES, SSM_CH)
        blocks = _sublane_transpose([g4[:, l_lo] for l_lo in range(SUBLANES)])
        for r, blk in enumerate(blocks):
            rows = slice(r * L + m * mh * SUBLANES, r * L + (m + 1) * mh * SUBLANES)
            z_ref[rows, :] = blk.reshape(mh * SUBLANES, SSM_CH).astype(BF16)


def _ssm(proj, wb, wct, a_all, d, weights, seq):
    t = proj.shape[0]
    nb = t // seq
    nk = seq // SSM_ROWS
    steps = SSM_GB * nb * nk
    tok = lambda g, b, k: (b * nk + k, g)
    par = lambda g, b, k: (g, 0, 0)
    step = lambda g, b, k: ((g * nb + b) * nk + k, 0)
    lh = SSM_CHUNK // SUBLANES
    cast_specs = [pl.BlockSpec((w.shape[0] // steps, w.shape[1]), step) for w in weights]
    return pl.pallas_call(
        functools.partial(_ssm_kernel, n_cast=len(weights)),
        grid=(SSM_GB, nb, nk),
        in_specs=[pl.BlockSpec((SSM_ROWS, SSM_CH), tok),
                  pl.BlockSpec((1, SSM_CH, 2 * SSM_ST), par),
                  pl.BlockSpec((1, SSM_CH, 2 * SSM_ST), par),
                  pl.BlockSpec((4, 1, SUBLANES, SSM_ST), lambda g, b, k: (0, g, 0, 0)),
                  pl.BlockSpec((1, 1, SSM_CH), par)] + cast_specs,
        out_specs=[pl.BlockSpec((SSM_ROWS, SSM_CH), tok)] + cast_specs,
        out_shape=[jax.ShapeDtypeStruct((t, SSM_WIDTH), BF16)]
        + [jax.ShapeDtypeStruct(w.shape, BF16) for w in weights],
        scratch_shapes=[pltpu.VMEM((lh, SUBLANES, SUBLANES, SSM_CH), F32),
                        pltpu.VMEM((SSM_ROWS, SSM_CH), BF16),
                        pltpu.VMEM((SSM_ROWS, 2 * SSM_ST), F32),
                        pltpu.VMEM((SSM_ROWS, 2 * SSM_ST), BF16),
                        pltpu.VMEM((SUBLANES, 2 * SSM_ST), F32),
                        pltpu.VMEM((SUBLANES, 2 * SSM_ST), F32)],
        compiler_params=_cparams(("arbitrary", "arbitrary", "arbitrary")),
        name="ssm",
    )(proj, wb, wct, a_all, d, *weights)


POOL_HALO = 32
POOL_LEAD = 16


def _pool_kernel(u_ref, w_ref, s_ref, o_ref, ext_ref, sum_ref, *, tt):
    k = pl.program_id(1)
    base = POOL_HALO - POOL_LEAD
    n = tt + POOL_LEAD

    @pl.when(k == 0)
    def _():
        ext_ref[0:POOL_HALO, :] = jnp.zeros((POOL_HALO, POOL_WIDTH), F32)

    @pl.when(k != 0)
    def _():
        ext_ref[0:POOL_HALO, :] = ext_ref[tt:tt + POOL_HALO, :]

    ext_ref[POOL_HALO:POOL_HALO + tt, :] = u_ref[...].astype(F32)
    sum_ref[0:base, :] = jnp.zeros((base, POOL_GROUP), F32)
    pos = (k * tt + lax.broadcasted_iota(jnp.int32, (tt, POOL_GROUP), 0)).astype(F32)
    for gi, w in enumerate(POOL_WINDOWS):
        cs = slice(gi * POOL_GROUP, (gi + 1) * POOL_GROUP)
        tot = ext_ref[base:base + n, cs] + ext_ref[base - 1:base - 1 + n, cs]
        m = 2
        while m < w:
            sum_ref[base:base + n, :] = tot
            tot = tot + sum_ref[base - m:base - m + n, :]
            m *= 2
        cur = ext_ref[POOL_HALO:POOL_HALO + tt, cs]
        count = jnp.minimum(pos + 1.0, float(w))
        pooled = tot[POOL_LEAD:, :] / count - cur
        y = jnp.dot(pooled.astype(BF16), w_ref[gi], preferred_element_type=F32)
        o_ref[:, cs] = (y * s_ref[:, cs]).astype(BF16)


def _pool(proj, w_pool, pool_scale, seq):
    tt = 512
    t = proj.shape[0]
    nk = seq // tt
    return pl.pallas_call(
        functools.partial(_pool_kernel, tt=tt),
        grid=(t // seq, nk),
        in_specs=[pl.BlockSpec((tt, POOL_WIDTH), lambda b, k: (b * nk + k, 1)),
                  pl.BlockSpec((len(POOL_WINDOWS), POOL_GROUP, POOL_GROUP),
                               lambda b, k: (0, 0, 0)),
                  pl.BlockSpec((1, POOL_WIDTH), lambda b, k: (0, 0))],
        out_specs=pl.BlockSpec((tt, POOL_WIDTH), lambda b, k: (b * nk + k, 0)),
        out_shape=jax.ShapeDtypeStruct((t, POOL_WIDTH), BF16),
        scratch_shapes=[pltpu.VMEM((tt + POOL_HALO, POOL_WIDTH), F32),
                        pltpu.VMEM((tt + POOL_HALO, POOL_GROUP), F32)],
        compiler_params=_cparams(("parallel", "arbitrary")),
        name="pool",
    )(proj, w_pool, pool_scale)


def _merge_kernel(z_ref, yp_ref, ga_ref, gb_ref, wv_ref, wg_ref, wp_ref, o_ref):
    half = z_ref.shape[0] // 2
    for s in range(2):
        rs = slice(s * half, (s + 1) * half)
        z = z_ref[rs, :]
        val = jnp.dot(z, wv_ref[...], preferred_element_type=F32)
        gate = jnp.dot(z, wg_ref[...], preferred_element_type=F32)
        y_a = val * jax.nn.sigmoid(gate)
        y_b = jnp.dot(yp_ref[rs, :], wp_ref[...], preferred_element_type=F32)
        merged = (jax.nn.sigmoid(ga_ref[rs, :].astype(F32)) * y_a
                  + jax.nn.sigmoid(gb_ref[rs, :].astype(F32)) * y_b)
        o_ref[rs, :] = merged.astype(BF16)


def _merge(z, yp, proj, w_val, w_gate, w_po):
    tm, tn = 1024, 1024
    t = z.shape[0]
    ga0 = (SSM_WIDTH + POOL_WIDTH) // tn
    gb0 = ga0 + D_MODEL // tn
    wspec = pl.BlockSpec((SSM_WIDTH, tn), lambda i, j: (0, j))
    return pl.pallas_call(
        _merge_kernel,
        grid=(t // tm, D_MODEL // tn),
        in_specs=[pl.BlockSpec((tm, SSM_WIDTH), lambda i, j: (i, 0)),
                  pl.BlockSpec((tm, POOL_WIDTH), lambda i, j: (i, 0)),
                  pl.BlockSpec((tm, tn), lambda i, j: (i, ga0 + j)),
                  pl.BlockSpec((tm, tn), lambda i, j: (i, gb0 + j)),
                  wspec, wspec, wspec],
        out_specs=pl.BlockSpec((tm, tn), lambda i, j: (i, j)),
        out_shape=jax.ShapeDtypeStruct((t, D_MODEL), BF16),
        compiler_params=_cparams(("parallel", "arbitrary")),
        name="merge",
    )(z, yp, proj, proj, w_val, w_gate, w_po)


def _out_ln_kernel(m_ref, w_ref, x_ref, g1_ref, lg_ref, lb_ref, sh_ref, sc_ref,
                   x1_ref, h2_ref):
    half = m_ref.shape[0] // 2
    for s in range(2):
        rs = slice(s * half, (s + 1) * half)
        y = jnp.dot(m_ref[rs, :], w_ref[...], preferred_element_type=F32)
        r = ALPHA * x_ref[rs, :] + g1_ref[0] * y
        x1 = _ln(r) * lg_ref[...] + lb_ref[...]
        x1_ref[rs, :] = x1
        h2_ref[rs, :] = (_ln(x1) * (1.0 + sc_ref[0]) + sh_ref[0]).astype(BF16)


def _out_ln(merged, w_out, x2, mod3, ln_g, ln_b, seq):
    tm = 512
    t = x2.shape[0]
    per_batch = seq // tm
    row = pl.BlockSpec((tm, D_MODEL), lambda i: (i, 0))
    vec = pl.BlockSpec((1, D_MODEL), lambda i: (0, 0))
    bvec = lambda col: pl.BlockSpec((1, 1, D_MODEL), lambda i: (i // per_batch, 0, col))
    return pl.pallas_call(
        _out_ln_kernel,
        grid=(t // tm,),
        in_specs=[row,
                  pl.BlockSpec((D_MODEL, D_MODEL), lambda i: (0, 0),
                               pipeline_mode=pl.Buffered(1)),
                  row, bvec(MOD_G1), vec, vec, bvec(MOD_SH2), bvec(MOD_SC2)],
        out_specs=[row, row],
        out_shape=[jax.ShapeDtypeStruct((t, D_MODEL), F32),
                   jax.ShapeDtypeStruct((t, D_MODEL), BF16)],
        compiler_params=_cparams(("parallel",)),
        name="out_ln",
    )(merged, w_out, x2, mod3, ln_g, ln_b, mod3, mod3)


def _mlp_kernel(h_ref, w1_ref, w2_ref, x1_ref, g2_ref, lg_ref, lb_ref, o_ref):
    f = pl.program_id(1)
    last = pl.num_programs(1) - 1
    half = h_ref.shape[0] // 2

    def ff(rows):
        a = jnp.maximum(jnp.dot(h_ref[rows, :], w1_ref[...],
                                preferred_element_type=F32), 0.0)
        return jnp.dot((a * a).astype(BF16), w2_ref[...], preferred_element_type=F32)

    @pl.when(f == 0)
    def _():
        o_ref[...] = ff(slice(None))

    @pl.when(jnp.logical_and(f > 0, f < last))
    def _():
        o_ref[...] += ff(slice(None))

    @pl.when(f == last)
    def _():
        for s in range(2):
            rs = slice(s * half, (s + 1) * half)
            r = ALPHA * x1_ref[rs, :] + g2_ref[0] * (o_ref[rs, :] + ff(rs))
            o_ref[rs, :] = _ln(r) * lg_ref[...] + lb_ref[...]


def _mlp(h2, w1, w2, x1, mod3, ln_g, ln_b, seq):
    tm, tf = 1024, 1024
    t = h2.shape[0]
    per_batch = seq // tm
    once = pl.Buffered(1)
    vec = pl.BlockSpec((1, D_MODEL), lambda i, f: (0, 0))
    return pl.pallas_call(
        _mlp_kernel,
        grid=(t // tm, D_FF // tf),
        in_specs=[pl.BlockSpec((tm, D_MODEL), lambda i, f: (i, 0), pipeline_mode=once),
                  pl.BlockSpec((D_MODEL, tf), lambda i, f: (0, f)),
                  pl.BlockSpec((tf, D_MODEL), lambda i, f: (f, 0)),
                  pl.BlockSpec((tm, D_MODEL), lambda i, f: (i, 0), pipeline_mode=once),
                  pl.BlockSpec((1, 1, D_MODEL), lambda i, f: (i // per_batch, 0, MOD_G2)),
                  vec, vec],
        out_specs=pl.BlockSpec((tm, D_MODEL), lambda i, f: (i, 0)),
        out_shape=jax.ShapeDtypeStruct((t, D_MODEL), F32),
        compiler_params=_cparams(("parallel", "arbitrary")),
        name="mlp",
    )(h2, w1, w2, x1, mod3, ln_g, ln_b)


def kernel(x, c, w_ada, b_ada, w_in, lam_re, lam_im, log_dt, ssm_b_re, ssm_b_im, ssm_c_re, ssm_c_im, ssm_d, w_glu_val, w_glu_gate, w_pool, pool_scale, w_pool_out, w_out, ln1_g, ln1_b, w_ff1, w_ff2, ln2_g, ln2_b):
    bsz, seq, d = x.shape
    t = bsz * seq
    l = 0
    x2 = x.reshape(t, d)

    c_pad = jnp.zeros((SUBLANES, d), F32).at[:bsz].set(c)
    mod3 = _mod(c_pad, w_ada[l], b_ada[l][None, :]).reshape(SUBLANES, 1, N_MOD * d)

    proj = _in_proj(x2, mod3, w_in[l].astype(BF16), seq)

    wb, wct, a_all = _ssm_prep(lam_re[l], lam_im[l], log_dt[l], ssm_b_re[l], ssm_b_im[l],
                               ssm_c_re[l], ssm_c_im[l])
    later = [w_ff1[l], w_ff2[l], w_out[l], w_glu_val[l], w_glu_gate[l], w_pool_out[l]]
    z, w1_bf, w2_bf, wo_bf, wv_bf, wg_bf, wp_bf = _ssm(
        proj, wb, wct, a_all, ssm_d[l].reshape(SSM_GB, 1, SSM_CH), later, seq)

    yp = _pool(proj, w_pool[l].astype(BF16), pool_scale[l][None, :], seq)

    merged = _merge(z, yp, proj, wv_bf, wg_bf, wp_bf)

    x1, h2 = _out_ln(merged, wo_bf, x2, mod3, ln1_g[l][None, :], ln1_b[l][None, :], seq)

    out = _mlp(h2, w1_bf, w2_bf, x1, mod3, ln2_g[l][None, :], ln2_b[l][None, :], seq)
    return out.reshape(bsz, seq, d)
```

```python
import functools
import math

import jax
import jax.numpy as jnp
from jax import lax
from jax.experimental import pallas as pl
from jax.experimental.pallas import tpu as pltpu

D_MODEL = 2048
SSM_WIDTH = 1024
SSM_GROUP = 16
SSM_GROUPS = 64
SSM_STATE = 64
POOL_WIDTH = 1024
POOL_WINDOWS = (2, 4, 8, 16)
POOL_GROUP = 256
IN_WIDTH = SSM_WIDTH + POOL_WIDTH + 2 * D_MODEL
D_FF = 4 * D_MODEL
N_MOD = 6
MOD_SH1, MOD_SC1, MOD_G1, MOD_SH2, MOD_SC2, MOD_G2 = range(N_MOD)
LN_EPS = 1e-5
ALPHA = 2.0 ** 0.25

SUBLANES = 8
LANES = 128
MXU_TILE = 256
VMEM_LIMIT = 62 * 1024 * 1024

MOD_TN = 1024
IN_PROJ_TM, IN_PROJ_TN = 1024, 2048
POOL_TT = 512
MERGE_TM, MERGE_TN = 1024, 1024
OUT_LN_TM = 512
MLP_TM, MLP_TF = 1024, 1024

SSM_CHUNK = 128
SSM_ROWS = SUBLANES * SSM_CHUNK
SSM_GB = 4
SSM_CH = SSM_WIDTH // SSM_GB
SSM_ST = SSM_CH // SSM_GROUP * SSM_STATE
SSM_COLS = 512

BF16 = jnp.bfloat16
F32 = jnp.float32


def _cparams(sem):
    return pltpu.CompilerParams(dimension_semantics=sem, vmem_limit_bytes=VMEM_LIMIT)


def _ln(x):
    mu = jnp.mean(x, axis=-1, keepdims=True)
    xc = x - mu
    var = jnp.mean(xc * xc, axis=-1, keepdims=True)
    return xc * lax.rsqrt(var + LN_EPS)


def _mod_kernel(c_ref, w_ref, b_ref, o_ref):
    c = c_ref[...]
    c_act = (c * jax.nn.sigmoid(c)).astype(BF16)
    o_ref[...] = jnp.dot(c_act, w_ref[...].astype(BF16),
                         preferred_element_type=F32) + b_ref[...]


def _mod(c_pad, w_ada, b_ada):
    tn = MOD_TN
    n = w_ada.shape[1]
    return pl.pallas_call(
        _mod_kernel,
        grid=(n // tn,),
        in_specs=[pl.BlockSpec((SUBLANES, D_MODEL), lambda j: (0, 0)),
                  pl.BlockSpec((D_MODEL, tn), lambda j: (0, j)),
                  pl.BlockSpec((1, tn), lambda j: (0, j))],
        out_specs=pl.BlockSpec((SUBLANES, tn), lambda j: (0, j)),
        out_shape=jax.ShapeDtypeStruct((SUBLANES, n), F32),
        compiler_params=_cparams(("arbitrary",)),
        name="mod",
    )(c_pad, w_ada, b_ada)


def _in_proj_kernel(x_ref, sh_ref, sc_ref, w_ref, proj_ref, h_ref):
    j = pl.program_id(1)
    half = x_ref.shape[0] // 2

    @pl.when(j == 0)
    def _():
        for s in range(2):
            rs = slice(s * half, (s + 1) * half)
            y = _ln(x_ref[rs, :])
            h_ref[rs, :] = (y * (1.0 + sc_ref[0]) + sh_ref[0]).astype(BF16)
            proj_ref[rs, :] = jnp.dot(h_ref[rs, :], w_ref[...],
                                      preferred_element_type=F32).astype(BF16)

    @pl.when(j != 0)
    def _():
        proj_ref[...] = jnp.dot(h_ref[...], w_ref[...],
                                preferred_element_type=F32).astype(BF16)


def _in_proj(x2, mod3, w_in, seq):
    tm, tn = IN_PROJ_TM, IN_PROJ_TN
    t = x2.shape[0]
    per_batch = seq // tm
    return pl.pallas_call(
        _in_proj_kernel,
        grid=(t // tm, IN_WIDTH // tn),
        in_specs=[pl.BlockSpec((tm, D_MODEL), lambda i, j: (i, 0)),
                  pl.BlockSpec((1, 1, D_MODEL), lambda i, j: (i // per_batch, 0, MOD_SH1)),
                  pl.BlockSpec((1, 1, D_MODEL), lambda i, j: (i // per_batch, 0, MOD_SC1)),
                  pl.BlockSpec((D_MODEL, tn), lambda i, j: (0, j))],
        out_specs=pl.BlockSpec((tm, tn), lambda i, j: (i, j)),
        out_shape=jax.ShapeDtypeStruct((t, IN_WIDTH), BF16),
        scratch_shapes=[pltpu.VMEM((tm, D_MODEL), BF16)],
        compiler_params=_cparams(("parallel", "arbitrary")),
        name="in_proj",
    )(x2, mod3, mod3, w_in)


def _zoh_pole(lr, li, dt, steps):
    mag = jnp.exp(lr * dt * steps)
    ang = li * dt * steps
    return mag * jnp.cos(ang), mag * jnp.sin(ang)


def _ssm_prep_kernel(lr_ref, li_ref, ldt_ref, br_ref, bi_ref, cr_ref, ci_ref,
                     lrs_ref, lis_ref, ldts_ref, wb_ref, wct_ref, a_ref):
    lr = lr_ref[...]
    li = li_ref[...]
    ab_re, ab_im = _zoh_pole(lr, li, jnp.exp(ldt_ref[...]), 1.0)
    num_re = ab_re - 1.0
    num_im = ab_im
    den = lr * lr + li * li
    f_re = (num_re * lr + num_im * li) / den
    f_im = (num_im * lr - num_re * li) / den
    br = br_ref[...]
    bi = bi_ref[...]
    bb_re = f_re * br - f_im * bi
    bb_im = f_re * bi + f_im * br

    state = lax.broadcasted_iota(jnp.int32, (SSM_STATE, SSM_ST), 0)
    lane = lax.broadcasted_iota(jnp.int32, (SSM_STATE, SSM_ST), 1)
    expand = ((lane & (SSM_STATE - 1)) == state).astype(BF16)
    row_g = lax.broadcasted_iota(jnp.int32, (SSM_CH, SSM_ST), 0) >> (SSM_GROUP.bit_length() - 1)
    col_g = lax.broadcasted_iota(jnp.int32, (SSM_CH, SSM_ST), 1) >> (SSM_STATE.bit_length() - 1)
    on_diag = row_g == col_g

    def place(w):
        t = jnp.dot(w.astype(BF16), expand, preferred_element_type=F32)
        return jnp.where(on_diag, t, 0.0).astype(BF16)

    wb_ref[0, :, 0:SSM_ST] = place(bb_re)
    wb_ref[0, :, SSM_ST:2 * SSM_ST] = place(bb_im)
    wct_ref[0, :, 0:SSM_ST] = place(cr_ref[...])
    wct_ref[0, :, SSM_ST:2 * SSM_ST] = place(-ci_ref[...])

    shape = (SUBLANES, SSM_ST)
    lrs = jnp.broadcast_to(lrs_ref[0], shape)
    lis = jnp.broadcast_to(lis_ref[0], shape)
    dts = jnp.exp(jnp.broadcast_to(ldts_ref[0], shape))
    a_ref[0, 0], a_ref[1, 0] = _zoh_pole(lrs, lis, dts, 1.0)
    a_ref[2, 0], a_ref[3, 0] = _zoh_pole(lrs, lis, dts, float(SSM_CHUNK))


def _ssm_prep(lam_re, lam_im, log_dt, b_re, b_im, c_re, c_im):
    rep = lambda a: jnp.repeat(a, SSM_GROUP, axis=0)
    ldt = jnp.broadcast_to(log_dt[:, None], (SSM_GROUPS, SSM_STATE))
    rows = [rep(lam_re), rep(lam_im), rep(ldt),
            b_re.transpose(0, 2, 1).reshape(SSM_WIDTH, SSM_STATE),
            b_im.transpose(0, 2, 1).reshape(SSM_WIDTH, SSM_STATE),
            c_re.reshape(SSM_WIDTH, SSM_STATE),
            c_im.reshape(SSM_WIDTH, SSM_STATE)]
    lanes = [a.reshape(SSM_GB, 1, SSM_ST) for a in (lam_re, lam_im, ldt)]
    row_spec = pl.BlockSpec((SSM_CH, SSM_STATE), lambda g: (g, 0))
    lane_spec = pl.BlockSpec((1, 1, SSM_ST), lambda g: (g, 0, 0))
    w_spec = pl.BlockSpec((1, SSM_CH, 2 * SSM_ST), lambda g: (g, 0, 0))
    w_shape = jax.ShapeDtypeStruct((SSM_GB, SSM_CH, 2 * SSM_ST), BF16)
    return pl.pallas_call(
        _ssm_prep_kernel,
        grid=(SSM_GB,),
        in_specs=[row_spec] * len(rows) + [lane_spec] * len(lanes),
        out_specs=[w_spec, w_spec,
                   pl.BlockSpec((4, 1, SUBLANES, SSM_ST), lambda g: (0, g, 0, 0))],
        out_shape=[w_shape, w_shape,
                   jax.ShapeDtypeStruct((4, SSM_GB, SUBLANES, SSM_ST), F32)],
        compiler_params=_cparams(("parallel",)),
        name="ssm_prep",
    )(*rows, *lanes)


def _sublane_transpose(blocks):
    sub = lax.broadcasted_iota(jnp.int32, blocks[0].shape, 1)
    a = list(blocks)
    for s in (1, 2, 4):
        keep = (sub & s) == 0
        nxt = list(a)
        for r in range(SUBLANES):
            if r & s == 0:
                lo, hi = a[r], a[r + s]
                nxt[r] = jnp.where(keep, lo, pltpu.roll(hi, s, 1))
                nxt[r + s] = jnp.where(keep, pltpu.roll(lo, SUBLANES - s, 1), hi)
        a = nxt
    return a


def _ssm_kernel(u_ref, wb_ref, wct_ref, a_ref, d_ref, *refs, n_cast):
    cast_in = refs[:n_cast]
    z_ref = refs[n_cast]
    cast_out = refs[n_cast + 1:2 * n_cast + 1]
    up_ref, upb_ref, bu_ref, xb_ref, cst_ref, st_ref = refs[2 * n_cast + 1:]
    k = pl.program_id(2)
    L = SSM_CHUNK
    lh = L // SUBLANES

    @pl.when(k == 0)
    def _():
        cst_ref[...] = jnp.zeros_like(cst_ref)

    u4 = u_ref[...].astype(F32).reshape(SUBLANES, lh, SUBLANES, SSM_CH)
    for l_lo, blk in enumerate(_sublane_transpose([u4[r] for r in range(SUBLANES)])):
        up_ref[:, l_lo] = blk
    upb_ref[...] = up_ref[...].reshape(SSM_ROWS, SSM_CH).astype(BF16)

    casts = list(zip(cast_in, cast_out))
    n_tiles = 2 * SSM_ST // MXU_TILE
    assert len(casts) <= n_tiles
    for n in range(n_tiles):
        cols = slice(n * MXU_TILE, (n + 1) * MXU_TILE)
        bu_ref[:, cols] = jnp.dot(upb_ref[...], wb_ref[0, :, cols],
                                  preferred_element_type=F32)
        if n < len(casts):
            src, dst = casts[n]
            dst[...] = src[...].astype(BF16)

    row_id = lax.broadcasted_iota(jnp.int32, (SUBLANES, SSM_COLS), 0)
    for cg in range(SSM_ST // SSM_COLS):
        re_sl = slice(cg * SSM_COLS, (cg + 1) * SSM_COLS)
        im_sl = slice(SSM_ST + cg * SSM_COLS, SSM_ST + (cg + 1) * SSM_COLS)
        ar = a_ref[0, 0, :, re_sl]
        ai = a_ref[1, 0, :, re_sl]

        def step(l, sr, si):
            row = pl.multiple_of(l * SUBLANES, SUBLANES)
            br = bu_ref[pl.ds(row, SUBLANES), re_sl]
            bi = bu_ref[pl.ds(row, SUBLANES), im_sl]
            return ar * sr - ai * si + br, ar * si + ai * sr + bi

        def pass1(l, carry):
            return step(l, *carry)
        er, ei = lax.fori_loop(0, L, pass1, (cst_ref[:, re_sl], cst_ref[:, im_sl]),
                               unroll=2)

        alr = a_ref[2, 0, :, re_sl]
        ali = a_ref[3, 0, :, re_sl]
        tr, ti = er, ei
        for r in range(1, SUBLANES):
            pr = pltpu.roll(tr, 1, 0)
            pi = pltpu.roll(ti, 1, 0)
            tr = jnp.where(row_id == r, er + alr * pr - ali * pi, tr)
            ti = jnp.where(row_id == r, ei + alr * pi + ali * pr, ti)
        pr = pltpu.roll(tr, 1, 0)
        pi = pltpu.roll(ti, 1, 0)
        first = row_id == 0
        st_ref[:, re_sl] = jnp.where(first, cst_ref[:, re_sl], pr)
        st_ref[:, im_sl] = jnp.where(first, cst_ref[:, im_sl], pi)
        cst_ref[:, re_sl] = jnp.where(first, pr, 0.0)
        cst_ref[:, im_sl] = jnp.where(first, pi, 0.0)

        def pass2(m, carry):
            sr, si = carry
            r1, i1 = step(2 * m, sr, si)
            r2, i2 = step(2 * m + 1, r1, i1)
            row = pl.multiple_of(m * 2 * SUBLANES, 2 * SUBLANES)
            xb_ref[pl.ds(row, 2 * SUBLANES), re_sl] = (
                jnp.concatenate([r1, r2], axis=0).astype(BF16))
            xb_ref[pl.ds(row, 2 * SUBLANES), im_sl] = (
                jnp.concatenate([i1, i2], axis=0).astype(BF16))
            return r2, i2
        lax.fori_loop(0, L // 2, pass2, (st_ref[:, re_sl], st_ref[:, im_sl]),
                      unroll=2)

    mrows = MXU_TILE
    mh = mrows // (SUBLANES * SUBLANES)
    for m in range(SSM_ROWS // mrows):
        rs = slice(m * mrows, (m + 1) * mrows)
        y = lax.dot_general(xb_ref[rs, :], wct_ref[0], (((1,), (1,)), ((), ())),
                            preferred_element_type=F32)
        y = y + d_ref[0] * up_ref[m * mh:(m + 1) * mh].reshape(mrows, SSM_CH)
        g4 = jax.nn.gelu(y, approximate=True).reshape(mh, SUBLANES, SUBLANES, SSM_CH)
        blocks = _sublane_transpose([g4[:, l_lo] for l_lo in range(SUBLANES)])
        for r, blk in enumerate(blocks):
            rows = slice(r * L + m * mh * SUBLANES, r * L + (m + 1) * mh * SUBLANES)
            z_ref[rows, :] = blk.reshape(mh * SUBLANES, SSM_CH).astype(BF16)


def _ssm(proj, wb, wct, a_all, d, weights, seq):
    t = proj.shape[0]
    nb = t // seq
    nk = seq // SSM_ROWS
    steps = SSM_GB * nb * nk
    tok = lambda g, b, k: (b * nk + k, g)
    par = lambda g, b, k: (g, 0, 0)
    step = lambda g, b, k: ((g * nb + b) * nk + k, 0)
    lh = SSM_CHUNK // SUBLANES
    cast_specs = [pl.BlockSpec((w.shape[0] // steps, w.shape[1]), step) for w in weights]
    return pl.pallas_call(
        functools.partial(_ssm_kernel, n_cast=len(weights)),
        grid=(SSM_GB, nb, nk),
        in_specs=[pl.BlockSpec((SSM_ROWS, SSM_CH), tok),
                  pl.BlockSpec((1, SSM_CH, 2 * SSM_ST), par),
                  pl.BlockSpec((1, SSM_CH, 2 * SSM_ST), par),
                  pl.BlockSpec((4, 1, SUBLANES, SSM_ST), lambda g, b, k: (0, g, 0, 0)),
                  pl.BlockSpec((1, 1, SSM_CH), par)] + cast_specs,
        out_specs=[pl.BlockSpec((SSM_ROWS, SSM_CH), tok)] + cast_specs,
        out_shape=[jax.ShapeDtypeStruct((t, SSM_WIDTH), BF16)]
        + [jax.ShapeDtypeStruct(w.shape, BF16) for w in weights],
        scratch_shapes=[pltpu.VMEM((lh, SUBLANES, SUBLANES, SSM_CH), F32),
                        pltpu.VMEM((SSM_ROWS, SSM_CH), BF16),
                        pltpu.VMEM((SSM_ROWS, 2 * SSM_ST), F32),
                        pltpu.VMEM((SSM_ROWS, 2 * SSM_ST), BF16),
                        pltpu.VMEM((SUBLANES, 2 * SSM_ST), F32),
                        pltpu.VMEM((SUBLANES, 2 * SSM_ST), F32)],
        compiler_params=_cparams(("arbitrary", "arbitrary", "arbitrary")),
        name="ssm",
    )(proj, wb, wct, a_all, d, *weights)


POOL_HALO = 32
POOL_LEAD = 16


def _pool_kernel(u_ref, w_ref, s_ref, o_ref, ext_ref, sum_ref, *, tt):
    k = pl.program_id(1)
    base = POOL_HALO - POOL_LEAD
    n = tt + POOL_LEAD

    @pl.when(k == 0)
    def _():
        ext_ref[0:POOL_HALO, :] = jnp.zeros((POOL_HALO, POOL_WIDTH), F32)

    @pl.when(k != 0)
    def _():
        ext_ref[0:POOL_HALO, :] = ext_ref[tt:tt + POOL_HALO, :]

    ext_ref[POOL_HALO:POOL_HALO + tt, :] = u_ref[...].astype(F32)
    sum_ref[0:base, :] = jnp.zeros((base, POOL_GROUP), F32)
    pos = (k * tt + lax.broadcasted_iota(jnp.int32, (tt, POOL_GROUP), 0)).astype(F32)
    for gi, w in enumerate(POOL_WINDOWS):
        cs = slice(gi * POOL_GROUP, (gi + 1) * POOL_GROUP)
        tot = ext_ref[base:base + n, cs] + ext_ref[base - 1:base - 1 + n, cs]
        m = 2
        while m < w:
            sum_ref[base:base + n, :] = tot
            tot = tot + sum_ref[base - m:base - m + n, :]
            m *= 2
        cur = ext_ref[POOL_HALO:POOL_HALO + tt, cs]
        count = jnp.minimum(pos + 1.0, float(w))
        pooled = tot[POOL_LEAD:, :] / count - cur
        y = jnp.dot(pooled.astype(BF16), w_ref[gi], preferred_element_type=F32)
        o_ref[:, cs] = (y * s_ref[:, cs]).astype(BF16)


def _pool(proj, w_pool, pool_scale, seq):
    tt = POOL_TT
    t = proj.shape[0]
    nk = seq // tt
    return pl.pallas_call(
        functools.partial(_pool_kernel, tt=tt),
        grid=(t // seq, nk),
        in_specs=[pl.BlockSpec((tt, POOL_WIDTH), lambda b, k: (b * nk + k, 1)),
                  pl.BlockSpec((len(POOL_WINDOWS), POOL_GROUP, POOL_GROUP),
                               lambda b, k: (0, 0, 0)),
                  pl.BlockSpec((1, POOL_WIDTH), lambda b, k: (0, 0))],
        out_specs=pl.BlockSpec((tt, POOL_WIDTH), lambda b, k: (b * nk + k, 0)),
        out_shape=jax.ShapeDtypeStruct((t, POOL_WIDTH), BF16),
        scratch_shapes=[pltpu.VMEM((tt + POOL_HALO, POOL_WIDTH), F32),
                        pltpu.VMEM((tt + POOL_HALO, POOL_GROUP), F32)],
        compiler_params=_cparams(("parallel", "arbitrary")),
        name="pool",
    )(proj, w_pool, pool_scale)


def _merge_kernel(z_ref, yp_ref, ga_ref, gb_ref, wv_ref, wg_ref, wp_ref, o_ref):
    half = z_ref.shape[0] // 2
    for s in range(2):
        rs = slice(s * half, (s + 1) * half)
        z = z_ref[rs, :]
        val = jnp.dot(z, wv_ref[...], preferred_element_type=F32)
        gate = jnp.dot(z, wg_ref[...], preferred_element_type=F32)
        y_a = val * jax.nn.sigmoid(gate)
        y_b = jnp.dot(yp_ref[rs, :], wp_ref[...], preferred_element_type=F32)
        merged = (jax.nn.sigmoid(ga_ref[rs, :].astype(F32)) * y_a
                  + jax.nn.sigmoid(gb_ref[rs, :].astype(F32)) * y_b)
        o_ref[rs, :] = merged.astype(BF16)


def _merge(z, yp, proj, w_val, w_gate, w_po):
    tm, tn = MERGE_TM, MERGE_TN
    t = z.shape[0]
    ga0 = (SSM_WIDTH + POOL_WIDTH) // tn
    gb0 = ga0 + D_MODEL // tn
    wspec = pl.BlockSpec((SSM_WIDTH, tn), lambda i, j: (0, j))
    return pl.pallas_call(
        _merge_kernel,
        grid=(t // tm, D_MODEL // tn),
        in_specs=[pl.BlockSpec((tm, SSM_WIDTH), lambda i, j: (i, 0)),
                  pl.BlockSpec((tm, POOL_WIDTH), lambda i, j: (i, 0)),
                  pl.BlockSpec((tm, tn), lambda i, j: (i, ga0 + j)),
                  pl.BlockSpec((tm, tn), lambda i, j: (i, gb0 + j)),
                  wspec, wspec, wspec],
        out_specs=pl.BlockSpec((tm, tn), lambda i, j: (i, j)),
        out_shape=jax.ShapeDtypeStruct((t, D_MODEL), BF16),
        compiler_params=_cparams(("parallel", "arbitrary")),
        name="merge",
    )(z, yp, proj, proj, w_val, w_gate, w_po)


def _out_ln_kernel(m_ref, w_ref, x_ref, g1_ref, lg_ref, lb_ref, sh_ref, sc_ref,
                   x1_ref, h2_ref):
    half = m_ref.shape[0] // 2
    for s in range(2):
        rs = slice(s * half, (s + 1) * half)
        y = jnp.dot(m_ref[rs, :], w_ref[...], preferred_element_type=F32)
        r = ALPHA * x_ref[rs, :] + g1_ref[0] * y
        x1 = _ln(r) * lg_ref[...] + lb_ref[...]
        x1_ref[rs, :] = x1
        h2_ref[rs, :] = (_ln(x1) * (1.0 + sc_ref[0]) + sh_ref[0]).astype(BF16)


def _out_ln(merged, w_out, x2, mod3, ln_g, ln_b, seq):
    tm = OUT_LN_TM
    t = x2.shape[0]
    per_batch = seq // tm
    row = pl.BlockSpec((tm, D_MODEL), lambda i: (i, 0))
    vec = pl.BlockSpec((1, D_MODEL), lambda i: (0, 0))
    bvec = lambda col: pl.BlockSpec((1, 1, D_MODEL), lambda i: (i // per_batch, 0, col))
    return pl.pallas_call(
        _out_ln_kernel,
        grid=(t // tm,),
        in_specs=[row,
                  pl.BlockSpec((D_MODEL, D_MODEL), lambda i: (0, 0),
                               pipeline_mode=pl.Buffered(1)),
                  row, bvec(MOD_G1), vec, vec, bvec(MOD_SH2), bvec(MOD_SC2)],
        out_specs=[row, row],
        out_shape=[jax.ShapeDtypeStruct((t, D_MODEL), F32),
                   jax.ShapeDtypeStruct((t, D_MODEL), BF16)],
        compiler_params=_cparams(("parallel",)),
        name="out_ln",
    )(merged, w_out, x2, mod3, ln_g, ln_b, mod3, mod3)


def _mlp_kernel(h_ref, w1_ref, w2_ref, x1_ref, g2_ref, lg_ref, lb_ref, o_ref):
    f = pl.program_id(1)
    last = pl.num_programs(1) - 1
    half = h_ref.shape[0] // 2

    def ff(rows):
        a = jnp.maximum(jnp.dot(h_ref[rows, :], w1_ref[...],
                                preferred_element_type=F32), 0.0)
        return jnp.dot((a * a).astype(BF16), w2_ref[...], preferred_element_type=F32)

    @pl.when(f == 0)
    def _():
        o_ref[...] = ff(slice(None))

    @pl.when(jnp.logical_and(f > 0, f < last))
    def _():
        o_ref[...] += ff(slice(None))

    @pl.when(f == last)
    def _():
        for s in range(2):
            rs = slice(s * half, (s + 1) * half)
            r = ALPHA * x1_ref[rs, :] + g2_ref[0] * (o_ref[rs, :] + ff(rs))
            o_ref[rs, :] = _ln(r) * lg_ref[...] + lb_ref[...]


def _mlp(h2, w1, w2, x1, mod3, ln_g, ln_b, seq):
    tm, tf = MLP_TM, MLP_TF
    t = h2.shape[0]
    per_batch = seq // tm
    once = pl.Buffered(1)
    vec = pl.BlockSpec((1, D_MODEL), lambda i, f: (0, 0))
    return pl.pallas_call(
        _mlp_kernel,
        grid=(t // tm, D_FF // tf),
        in_specs=[pl.BlockSpec((tm, D_MODEL), lambda i, f: (i, 0), pipeline_mode=once),
                  pl.BlockSpec((D_MODEL, tf), lambda i, f: (0, f)),
                  pl.BlockSpec((tf, D_MODEL), lambda i, f: (f, 0)),
                  pl.BlockSpec((tm, D_MODEL), lambda i, f: (i, 0), pipeline_mode=once),
                  pl.BlockSpec((1, 1, D_MODEL), lambda i, f: (i // per_batch, 0, MOD_G2)),
                  vec, vec],
        out_specs=pl.BlockSpec((tm, D_MODEL), lambda i, f: (i, 0)),
        out_shape=jax.ShapeDtypeStruct((t, D_MODEL), F32),
        compiler_params=_cparams(("parallel", "arbitrary")),
        name="mlp",
    )(h2, w1, w2, x1, mod3, ln_g, ln_b)


def kernel(x, c, w_ada, b_ada, w_in, lam_re, lam_im, log_dt, ssm_b_re, ssm_b_im, ssm_c_re, ssm_c_im, ssm_d, w_glu_val, w_glu_gate, w_pool, pool_scale, w_pool_out, w_out, ln1_g, ln1_b, w_ff1, w_ff2, ln2_g, ln2_b):
    bsz, seq, d = x.shape
    t = bsz * seq
    l = 0
    x2 = x.reshape(t, d)

    c_pad = jnp.zeros((SUBLANES, d), F32).at[:bsz].set(c)
    mod3 = _mod(c_pad, w_ada[l], b_ada[l][None, :]).reshape(SUBLANES, 1, N_MOD * d)

    proj = _in_proj(x2, mod3, w_in[l].astype(BF16), seq)

    wb, wct, a_all = _ssm_prep(lam_re[l], lam_im[l], log_dt[l], ssm_b_re[l], ssm_b_im[l],
                               ssm_c_re[l], ssm_c_im[l])
    later = [w_ff1[l], w_ff2[l], w_out[l], w_glu_val[l], w_glu_gate[l], w_pool_out[l]]
    z, w1_bf, w2_bf, wo_bf, wv_bf, wg_bf, wp_bf = _ssm(
        proj, wb, wct, a_all, ssm_d[l].reshape(SSM_GB, 1, SSM_CH), later, seq)

    yp = _pool(proj, w_pool[l].astype(BF16), pool_scale[l][None, :], seq)

    merged = _merge(z, yp, proj, wv_bf, wg_bf, wp_bf)

    x1, h2 = _out_ln(merged, wo_bf, x2, mod3, ln1_g[l][None, :], ln1_b[l][None, :], seq)

    out = _mlp(h2, w1_bf, w2_bf, x1, mod3, ln2_g[l][None, :], ln2_b[l][None, :], seq)
    return out.reshape(bsz, seq, d)
```

```python
import functools
import math

import jax
import jax.numpy as jnp
from jax import lax
from jax.experimental import pallas as pl
from jax.experimental.pallas import tpu as pltpu

D_MODEL = 2048
SSM_WIDTH = 1024
SSM_GROUP = 16
SSM_GROUPS = 64
SSM_STATE = 64
POOL_WIDTH = 1024
POOL_WINDOWS = (2, 4, 8, 16)
POOL_GROUP = 256
IN_WIDTH = SSM_WIDTH + POOL_WIDTH + 2 * D_MODEL
D_FF = 4 * D_MODEL
N_MOD = 6
MOD_SH1, MOD_SC1, MOD_G1, MOD_SH2, MOD_SC2, MOD_G2 = range(N_MOD)
LN_EPS = 1e-5
ALPHA = 2.0 ** 0.25

SUBLANES = 8
LANES = 128
MXU_TILE = 256
VMEM_LIMIT = 62 * 1024 * 1024

MOD_TN = 1024
IN_PROJ_TM, IN_PROJ_TN = 1024, 2048
POOL_TT = 512
MERGE_TM, MERGE_TN = 1024, 1024
OUT_LN_TM, OUT_LN_PART = 1024, 256
MLP_TM, MLP_TF = 1024, 1024

SSM_CHUNK = 128
SSM_ROWS = SUBLANES * SSM_CHUNK
SSM_GB = 4
SSM_CH = SSM_WIDTH // SSM_GB
SSM_ST = SSM_CH // SSM_GROUP * SSM_STATE
SSM_COLS = 512

BF16 = jnp.bfloat16
F32 = jnp.float32


def _cparams(sem):
    return pltpu.CompilerParams(dimension_semantics=sem, vmem_limit_bytes=VMEM_LIMIT)


def _ln(x):
    mu = jnp.mean(x, axis=-1, keepdims=True)
    xc = x - mu
    var = jnp.mean(xc * xc, axis=-1, keepdims=True)
    return xc * lax.rsqrt(var + LN_EPS)


def _mod_kernel(c_ref, w_ref, b_ref, o_ref):
    c = c_ref[...]
    c_act = (c * jax.nn.sigmoid(c)).astype(BF16)
    o_ref[...] = jnp.dot(c_act, w_ref[...].astype(BF16),
                         preferred_element_type=F32) + b_ref[...]


def _mod(c_pad, w_ada, b_ada):
    tn = MOD_TN
    n = w_ada.shape[1]
    return pl.pallas_call(
        _mod_kernel,
        grid=(n // tn,),
        in_specs=[pl.BlockSpec((SUBLANES, D_MODEL), lambda j: (0, 0)),
                  pl.BlockSpec((D_MODEL, tn), lambda j: (0, j)),
                  pl.BlockSpec((1, tn), lambda j: (0, j))],
        out_specs=pl.BlockSpec((SUBLANES, tn), lambda j: (0, j)),
        out_shape=jax.ShapeDtypeStruct((SUBLANES, n), F32),
        compiler_params=_cparams(("arbitrary",)),
        name="mod",
    )(c_pad, w_ada, b_ada)


def _in_proj_kernel(x_ref, sh_ref, sc_ref, w_ref, proj_ref, h_ref):
    j = pl.program_id(1)
    half = x_ref.shape[0] // 2

    @pl.when(j == 0)
    def _():
        for s in range(2):
            rs = slice(s * half, (s + 1) * half)
            y = _ln(x_ref[rs, :])
            h_ref[rs, :] = (y * (1.0 + sc_ref[0]) + sh_ref[0]).astype(BF16)
            proj_ref[rs, :] = jnp.dot(h_ref[rs, :], w_ref[...],
                                      preferred_element_type=F32).astype(BF16)

    @pl.when(j != 0)
    def _():
        proj_ref[...] = jnp.dot(h_ref[...], w_ref[...],
                                preferred_element_type=F32).astype(BF16)


def _in_proj(x2, mod3, w_in, seq):
    tm, tn = IN_PROJ_TM, IN_PROJ_TN
    t = x2.shape[0]
    per_batch = seq // tm
    return pl.pallas_call(
        _in_proj_kernel,
        grid=(t // tm, IN_WIDTH // tn),
        in_specs=[pl.BlockSpec((tm, D_MODEL), lambda i, j: (i, 0)),
                  pl.BlockSpec((1, 1, D_MODEL), lambda i, j: (i // per_batch, 0, MOD_SH1)),
                  pl.BlockSpec((1, 1, D_MODEL), lambda i, j: (i // per_batch, 0, MOD_SC1)),
                  pl.BlockSpec((D_MODEL, tn), lambda i, j: (0, j))],
        out_specs=pl.BlockSpec((tm, tn), lambda i, j: (i, j)),
        out_shape=jax.ShapeDtypeStruct((t, IN_WIDTH), BF16),
        scratch_shapes=[pltpu.VMEM((tm, D_MODEL), BF16)],
        compiler_params=_cparams(("parallel", "arbitrary")),
        name="in_proj",
    )(x2, mod3, mod3, w_in)


def _zoh_pole(lr, li, dt, steps):
    mag = jnp.exp(lr * dt * steps)
    ang = li * dt * steps
    return mag * jnp.cos(ang), mag * jnp.sin(ang)


def _ssm_prep_kernel(lr_ref, li_ref, ldt_ref, br_ref, bi_ref, cr_ref, ci_ref,
                     lrs_ref, lis_ref, ldts_ref, wb_ref, wct_ref, a_ref):
    lr = lr_ref[...]
    li = li_ref[...]
    ab_re, ab_im = _zoh_pole(lr, li, jnp.exp(ldt_ref[...]), 1.0)
    num_re = ab_re - 1.0
    num_im = ab_im
    den = lr * lr + li * li
    f_re = (num_re * lr + num_im * li) / den
    f_im = (num_im * lr - num_re * li) / den
    br = br_ref[...]
    bi = bi_ref[...]
    bb_re = f_re * br - f_im * bi
    bb_im = f_re * bi + f_im * br

    state = lax.broadcasted_iota(jnp.int32, (SSM_STATE, SSM_ST), 0)
    lane = lax.broadcasted_iota(jnp.int32, (SSM_STATE, SSM_ST), 1)
    expand = ((lane & (SSM_STATE - 1)) == state).astype(BF16)
    row_g = lax.broadcasted_iota(jnp.int32, (SSM_CH, SSM_ST), 0) >> (SSM_GROUP.bit_length() - 1)
    col_g = lax.broadcasted_iota(jnp.int32, (SSM_CH, SSM_ST), 1) >> (SSM_STATE.bit_length() - 1)
    on_diag = row_g == col_g

    def place(w):
        t = jnp.dot(w.astype(BF16), expand, preferred_element_type=F32)
        return jnp.where(on_diag, t, 0.0).astype(BF16)

    wb_ref[0, :, 0:SSM_ST] = place(bb_re)
    wb_ref[0, :, SSM_ST:2 * SSM_ST] = place(bb_im)
    wct_ref[0, :, 0:SSM_ST] = place(cr_ref[...])
    wct_ref[0, :, SSM_ST:2 * SSM_ST] = place(-ci_ref[...])

    shape = (SUBLANES, SSM_ST)
    lrs = jnp.broadcast_to(lrs_ref[0], shape)
    lis = jnp.broadcast_to(lis_ref[0], shape)
    dts = jnp.exp(jnp.broadcast_to(ldts_ref[0], shape))
    a_ref[0, 0], a_ref[1, 0] = _zoh_pole(lrs, lis, dts, 1.0)
    a_ref[2, 0], a_ref[3, 0] = _zoh_pole(lrs, lis, dts, float(SSM_CHUNK))


def _ssm_prep(lam_re, lam_im, log_dt, b_re, b_im, c_re, c_im):
    rep = lambda a: jnp.repeat(a, SSM_GROUP, axis=0)
    ldt = jnp.broadcast_to(log_dt[:, None], (SSM_GROUPS, SSM_STATE))
    rows = [rep(lam_re), rep(lam_im), rep(ldt),
            b_re.transpose(0, 2, 1).reshape(SSM_WIDTH, SSM_STATE),
            b_im.transpose(0, 2, 1).reshape(SSM_WIDTH, SSM_STATE),
            c_re.reshape(SSM_WIDTH, SSM_STATE),
            c_im.reshape(SSM_WIDTH, SSM_STATE)]
    lanes = [a.reshape(SSM_GB, 1, SSM_ST) for a in (lam_re, lam_im, ldt)]
    row_spec = pl.BlockSpec((SSM_CH, SSM_STATE), lambda g: (g, 0))
    lane_spec = pl.BlockSpec((1, 1, SSM_ST), lambda g: (g, 0, 0))
    w_spec = pl.BlockSpec((1, SSM_CH, 2 * SSM_ST), lambda g: (g, 0, 0))
    w_shape = jax.ShapeDtypeStruct((SSM_GB, SSM_CH, 2 * SSM_ST), BF16)
    return pl.pallas_call(
        _ssm_prep_kernel,
        grid=(SSM_GB,),
        in_specs=[row_spec] * len(rows) + [lane_spec] * len(lanes),
        out_specs=[w_spec, w_spec,
                   pl.BlockSpec((4, 1, SUBLANES, SSM_ST), lambda g: (0, g, 0, 0))],
        out_shape=[w_shape, w_shape,
                   jax.ShapeDtypeStruct((4, SSM_GB, SUBLANES, SSM_ST), F32)],
        compiler_params=_cparams(("parallel",)),
        name="ssm_prep",
    )(*rows, *lanes)


def _sublane_transpose(blocks):
    sub = lax.broadcasted_iota(jnp.int32, blocks[0].shape, 1)
    a = list(blocks)
    for s in (1, 2, 4):
        keep = (sub & s) == 0
        nxt = list(a)
        for r in range(SUBLANES):
            if r & s == 0:
                lo, hi = a[r], a[r + s]
                nxt[r] = jnp.where(keep, lo, pltpu.roll(hi, s, 1))
                nxt[r + s] = jnp.where(keep, pltpu.roll(lo, SUBLANES - s, 1), hi)
        a = nxt
    return a


def _ssm_kernel(u_ref, wb_ref, wct_ref, a_ref, d_ref, *refs, n_cast):
    cast_in = refs[:n_cast]
    z_ref = refs[n_cast]
    cast_out = refs[n_cast + 1:2 * n_cast + 1]
    up_ref, upb_ref, bu_ref, xb_ref, cst_ref, st_ref = refs[2 * n_cast + 1:]
    k = pl.program_id(2)
    L = SSM_CHUNK
    lh = L // SUBLANES

    @pl.when(k == 0)
    def _():
        cst_ref[...] = jnp.zeros_like(cst_ref)

    u4 = u_ref[...].astype(F32).reshape(SUBLANES, lh, SUBLANES, SSM_CH)
    for l_lo, blk in enumerate(_sublane_transpose([u4[r] for r in range(SUBLANES)])):
        up_ref[:, l_lo] = blk
    upb_ref[...] = up_ref[...].reshape(SSM_ROWS, SSM_CH).astype(BF16)

    casts = list(zip(cast_in, cast_out))
    n_tiles = 2 * SSM_ST // MXU_TILE
    assert len(casts) <= n_tiles
    for n in range(n_tiles):
        cols = slice(n * MXU_TILE, (n + 1) * MXU_TILE)
        bu_ref[:, cols] = jnp.dot(upb_ref[...], wb_ref[0, :, cols],
                                  preferred_element_type=F32)
        if n < len(casts):
            src, dst = casts[n]
            dst[...] = src[...].astype(BF16)

    row_id = lax.broadcasted_iota(jnp.int32, (SUBLANES, SSM_COLS), 0)
    for cg in range(SSM_ST // SSM_COLS):
        re_sl = slice(cg * SSM_COLS, (cg + 1) * SSM_COLS)
        im_sl = slice(SSM_ST + cg * SSM_COLS, SSM_ST + (cg + 1) * SSM_COLS)
        ar = a_ref[0, 0, :, re_sl]
        ai = a_ref[1, 0, :, re_sl]

        def step(l, sr, si):
            row = pl.multiple_of(l * SUBLANES, SUBLANES)
            br = bu_ref[pl.ds(row, SUBLANES), re_sl]
            bi = bu_ref[pl.ds(row, SUBLANES), im_sl]
            return ar * sr - ai * si + br, ar * si + ai * sr + bi

        def pass1(l, carry):
            return step(l, *carry)
        er, ei = lax.fori_loop(0, L, pass1, (cst_ref[:, re_sl], cst_ref[:, im_sl]),
                               unroll=2)

        alr = a_ref[2, 0, :, re_sl]
        ali = a_ref[3, 0, :, re_sl]
        tr, ti = er, ei
        for r in range(1, SUBLANES):
            pr = pltpu.roll(tr, 1, 0)
            pi = pltpu.roll(ti, 1, 0)
            tr = jnp.where(row_id == r, er + alr * pr - ali * pi, tr)
            ti = jnp.where(row_id == r, ei + alr * pi + ali * pr, ti)
        pr = pltpu.roll(tr, 1, 0)
        pi = pltpu.roll(ti, 1, 0)
        first = row_id == 0
        st_ref[:, re_sl] = jnp.where(first, cst_ref[:, re_sl], pr)
        st_ref[:, im_sl] = jnp.where(first, cst_ref[:, im_sl], pi)
        cst_ref[:, re_sl] = jnp.where(first, pr, 0.0)
        cst_ref[:, im_sl] = jnp.where(first, pi, 0.0)

        def pass2(m, carry):
            sr, si = carry
            r1, i1 = step(2 * m, sr, si)
            r2, i2 = step(2 * m + 1, r1, i1)
            row = pl.multiple_of(m * 2 * SUBLANES, 2 * SUBLANES)
            xb_ref[pl.ds(row, 2 * SUBLANES), re_sl] = (
                jnp.concatenate([r1, r2], axis=0).astype(BF16))
            xb_ref[pl.ds(row, 2 * SUBLANES), im_sl] = (
                jnp.concatenate([i1, i2], axis=0).astype(BF16))
            return r2, i2
        lax.fori_loop(0, L // 2, pass2, (st_ref[:, re_sl], st_ref[:, im_sl]),
                      unroll=2)

    mrows = MXU_TILE
    mh = mrows // (SUBLANES * SUBLANES)
    for m in range(SSM_ROWS // mrows):
        rs = slice(m * mrows, (m + 1) * mrows)
        y = lax.dot_general(xb_ref[rs, :], wct_ref[0], (((1,), (1,)), ((), ())),
                            preferred_element_type=F32)
        y = y + d_ref[0] * up_ref[m * mh:(m + 1) * mh].reshape(mrows, SSM_CH)
        g4 = jax.nn.gelu(y, approximate=True).reshape(mh, SUBLANES, SUBLANES, SSM_CH)
        blocks = _sublane_transpose([g4[:, l_lo] for l_lo in range(SUBLANES)])
        for r, blk in enumerate(blocks):
            rows = slice(r * L + m * mh * SUBLANES, r * L + (m + 1) * mh * SUBLANES)
            z_ref[rows, :] = blk.reshape(mh * SUBLANES, SSM_CH).astype(BF16)


def _ssm(proj, wb, wct, a_all, d, weights, seq):
    t = proj.shape[0]
    nb = t // seq
    nk = seq // SSM_ROWS
    steps = SSM_GB * nb * nk
    tok = lambda g, b, k: (b * nk + k, g)
    par = lambda g, b, k: (g, 0, 0)
    step = lambda g, b, k: ((g * nb + b) * nk + k, 0)
    lh = SSM_CHUNK // SUBLANES
    cast_specs = [pl.BlockSpec((w.shape[0] // steps, w.shape[1]), step) for w in weights]
    return pl.pallas_call(
        functools.partial(_ssm_kernel, n_cast=len(weights)),
        grid=(SSM_GB, nb, nk),
        in_specs=[pl.BlockSpec((SSM_ROWS, SSM_CH), tok),
                  pl.BlockSpec((1, SSM_CH, 2 * SSM_ST), par),
                  pl.BlockSpec((1, SSM_CH, 2 * SSM_ST), par),
                  pl.BlockSpec((4, 1, SUBLANES, SSM_ST), lambda g, b, k: (0, g, 0, 0)),
                  pl.BlockSpec((1, 1, SSM_CH), par)] + cast_specs,
        out_specs=[pl.BlockSpec((SSM_ROWS, SSM_CH), tok)] + cast_specs,
        out_shape=[jax.ShapeDtypeStruct((t, SSM_WIDTH), BF16)]
        + [jax.ShapeDtypeStruct(w.shape, BF16) for w in weights],
        scratch_shapes=[pltpu.VMEM((lh, SUBLANES, SUBLANES, SSM_CH), F32),
                        pltpu.VMEM((SSM_ROWS, SSM_CH), BF16),
                        pltpu.VMEM((SSM_ROWS, 2 * SSM_ST), F32),
                        pltpu.VMEM((SSM_ROWS, 2 * SSM_ST), BF16),
                        pltpu.VMEM((SUBLANES, 2 * SSM_ST), F32),
                        pltpu.VMEM((SUBLANES, 2 * SSM_ST), F32)],
        compiler_params=_cparams(("arbitrary", "arbitrary", "arbitrary")),
        name="ssm",
    )(proj, wb, wct, a_all, d, *weights)


POOL_HALO = 32
POOL_LEAD = 16


def _pool_kernel(u_ref, w_ref, s_ref, o_ref, ext_ref, sum_ref, *, tt):
    k = pl.program_id(1)
    base = POOL_HALO - POOL_LEAD
    n = tt + POOL_LEAD

    @pl.when(k == 0)
    def _():
        ext_ref[0:POOL_HALO, :] = jnp.zeros((POOL_HALO, POOL_WIDTH), F32)

    @pl.when(k != 0)
    def _():
        ext_ref[0:POOL_HALO, :] = ext_ref[tt:tt + POOL_HALO, :]

    ext_ref[POOL_HALO:POOL_HALO + tt, :] = u_ref[...].astype(F32)
    sum_ref[0:base, :] = jnp.zeros((base, POOL_GROUP), F32)
    pos = (k * tt + lax.broadcasted_iota(jnp.int32, (tt, POOL_GROUP), 0)).astype(F32)
    for gi, w in enumerate(POOL_WINDOWS):
        cs = slice(gi * POOL_GROUP, (gi + 1) * POOL_GROUP)
        tot = ext_ref[base:base + n, cs] + ext_ref[base - 1:base - 1 + n, cs]
        m = 2
        while m < w:
            sum_ref[base:base + n, :] = tot
            tot = tot + sum_ref[base - m:base - m + n, :]
            m *= 2
        cur = ext_ref[POOL_HALO:POOL_HALO + tt, cs]
        count = jnp.minimum(pos + 1.0, float(w))
        pooled = tot[POOL_LEAD:, :] / count - cur
        y = jnp.dot(pooled.astype(BF16), w_ref[gi], preferred_element_type=F32)
        o_ref[:, cs] = (y * s_ref[:, cs]).astype(BF16)


def _pool(proj, w_pool, pool_scale, seq):
    tt = POOL_TT
    t = proj.shape[0]
    nk = seq // tt
    return pl.pallas_call(
        functools.partial(_pool_kernel, tt=tt),
        grid=(t // seq, nk),
        in_specs=[pl.BlockSpec((tt, POOL_WIDTH), lambda b, k: (b * nk + k, 1)),
                  pl.BlockSpec((len(POOL_WINDOWS), POOL_GROUP, POOL_GROUP),
                               lambda b, k: (0, 0, 0)),
                  pl.BlockSpec((1, POOL_WIDTH), lambda b, k: (0, 0))],
        out_specs=pl.BlockSpec((tt, POOL_WIDTH), lambda b, k: (b * nk + k, 0)),
        out_shape=jax.ShapeDtypeStruct((t, POOL_WIDTH), BF16),
        scratch_shapes=[pltpu.VMEM((tt + POOL_HALO, POOL_WIDTH), F32),
                        pltpu.VMEM((tt + POOL_HALO, POOL_GROUP), F32)],
        compiler_params=_cparams(("parallel", "arbitrary")),
        name="pool",
    )(proj, w_pool, pool_scale)


def _merge_kernel(z_ref, yp_ref, ga_ref, gb_ref, wv_ref, wg_ref, wp_ref, o_ref):
    half = z_ref.shape[0] // 2
    for s in range(2):
        rs = slice(s * half, (s + 1) * half)
        z = z_ref[rs, :]
        val = jnp.dot(z, wv_ref[...], preferred_element_type=F32)
        gate = jnp.dot(z, wg_ref[...], preferred_element_type=F32)
        y_a = val * jax.nn.sigmoid(gate)
        y_b = jnp.dot(yp_ref[rs, :], wp_ref[...], preferred_element_type=F32)
        merged = (jax.nn.sigmoid(ga_ref[rs, :].astype(F32)) * y_a
                  + jax.nn.sigmoid(gb_ref[rs, :].astype(F32)) * y_b)
        o_ref[rs, :] = merged.astype(BF16)


def _merge(z, yp, proj, w_val, w_gate, w_po):
    tm, tn = MERGE_TM, MERGE_TN
    t = z.shape[0]
    ga0 = (SSM_WIDTH + POOL_WIDTH) // tn
    gb0 = ga0 + D_MODEL // tn
    wspec = pl.BlockSpec((SSM_WIDTH, tn), lambda i, j: (0, j))
    return pl.pallas_call(
        _merge_kernel,
        grid=(t // tm, D_MODEL // tn),
        in_specs=[pl.BlockSpec((tm, SSM_WIDTH), lambda i, j: (i, 0)),
                  pl.BlockSpec((tm, POOL_WIDTH), lambda i, j: (i, 0)),
                  pl.BlockSpec((tm, tn), lambda i, j: (i, ga0 + j)),
                  pl.BlockSpec((tm, tn), lambda i, j: (i, gb0 + j)),
                  wspec, wspec, wspec],
        out_specs=pl.BlockSpec((tm, tn), lambda i, j: (i, j)),
        out_shape=jax.ShapeDtypeStruct((t, D_MODEL), BF16),
        compiler_params=_cparams(("parallel", "arbitrary")),
        name="merge",
    )(z, yp, proj, proj, w_val, w_gate, w_po)


def _out_ln_kernel(m_ref, w_ref, x_ref, g1_ref, lg_ref, lb_ref, sh_ref, sc_ref,
                   x1_ref, h2_ref):
    half = OUT_LN_PART
    for s in range(m_ref.shape[0] // half):
        rs = slice(s * half, (s + 1) * half)
        y = jnp.dot(m_ref[rs, :], w_ref[...], preferred_element_type=F32)
        r = ALPHA * x_ref[rs, :] + g1_ref[0] * y
        x1 = _ln(r) * lg_ref[...] + lb_ref[...]
        x1_ref[rs, :] = x1
        h2_ref[rs, :] = (_ln(x1) * (1.0 + sc_ref[0]) + sh_ref[0]).astype(BF16)


def _out_ln(merged, w_out, x2, mod3, ln_g, ln_b, seq):
    tm = OUT_LN_TM
    t = x2.shape[0]
    per_batch = seq // tm
    row = pl.BlockSpec((tm, D_MODEL), lambda i: (i, 0))
    vec = pl.BlockSpec((1, D_MODEL), lambda i: (0, 0))
    bvec = lambda col: pl.BlockSpec((1, 1, D_MODEL), lambda i: (i // per_batch, 0, col))
    return pl.pallas_call(
        _out_ln_kernel,
        grid=(t // tm,),
        in_specs=[row,
                  pl.BlockSpec((D_MODEL, D_MODEL), lambda i: (0, 0),
                               pipeline_mode=pl.Buffered(1)),
                  row, bvec(MOD_G1), vec, vec, bvec(MOD_SH2), bvec(MOD_SC2)],
        out_specs=[row, row],
        out_shape=[jax.ShapeDtypeStruct((t, D_MODEL), F32),
                   jax.ShapeDtypeStruct((t, D_MODEL), BF16)],
        compiler_params=_cparams(("parallel",)),
        name="out_ln",
    )(merged, w_out, x2, mod3, ln_g, ln_b, mod3, mod3)


def _mlp_kernel(h_ref, w1_ref, w2_ref, x1_ref, g2_ref, lg_ref, lb_ref, o_ref):
    f = pl.program_id(1)
    last = pl.num_programs(1) - 1
    half = h_ref.shape[0] // 2

    def ff(rows):
        a = jnp.maximum(jnp.dot(h_ref[rows, :], w1_ref[...],
                                preferred_element_type=F32), 0.0)
        return jnp.dot((a * a).astype(BF16), w2_ref[...], preferred_element_type=F32)

    @pl.when(f == 0)
    def _():
        o_ref[...] = ff(slice(None))

    @pl.when(jnp.logical_and(f > 0, f < last))
    def _():
        o_ref[...] += ff(slice(None))

    @pl.when(f == last)
    def _():
        for s in range(2):
            rs = slice(s * half, (s + 1) * half)
            r = ALPHA * x1_ref[rs, :] + g2_ref[0] * (o_ref[rs, :] + ff(rs))
            o_ref[rs, :] = _ln(r) * lg_ref[...] + lb_ref[...]


def _mlp(h2, w1, w2, x1, mod3, ln_g, ln_b, seq):
    tm, tf = MLP_TM, MLP_TF
    t = h2.shape[0]
    per_batch = seq // tm
    once = pl.Buffered(1)
    vec = pl.BlockSpec((1, D_MODEL), lambda i, f: (0, 0))
    return pl.pallas_call(
        _mlp_kernel,
        grid=(t // tm, D_FF // tf),
        in_specs=[pl.BlockSpec((tm, D_MODEL), lambda i, f: (i, 0), pipeline_mode=once),
                  pl.BlockSpec((D_MODEL, tf), lambda i, f: (0, f)),
                  pl.BlockSpec((tf, D_MODEL), lambda i, f: (f, 0)),
                  pl.BlockSpec((tm, D_MODEL), lambda i, f: (i, 0), pipeline_mode=once),
                  pl.BlockSpec((1, 1, D_MODEL), lambda i, f: (i // per_batch, 0, MOD_G2)),
                  vec, vec],
        out_specs=pl.BlockSpec((tm, D_MODEL), lambda i, f: (i, 0)),
        out_shape=jax.ShapeDtypeStruct((t, D_MODEL), F32),
        compiler_params=_cparams(("parallel", "arbitrary")),
        name="mlp",
    )(h2, w1, w2, x1, mod3, ln_g, ln_b)


def kernel(x, c, w_ada, b_ada, w_in, lam_re, lam_im, log_dt, ssm_b_re, ssm_b_im, ssm_c_re, ssm_c_im, ssm_d, w_glu_val, w_glu_gate, w_pool, pool_scale, w_pool_out, w_out, ln1_g, ln1_b, w_ff1, w_ff2, ln2_g, ln2_b):
    bsz, seq, d = x.shape
    t = bsz * seq
    l = 0
    x2 = x.reshape(t, d)

    c_pad = jnp.zeros((SUBLANES, d), F32).at[:bsz].set(c)
    mod3 = _mod(c_pad, w_ada[l], b_ada[l][None, :]).reshape(SUBLANES, 1, N_MOD * d)

    proj = _in_proj(x2, mod3, w_in[l].astype(BF16), seq)

    wb, wct, a_all = _ssm_prep(lam_re[l], lam_im[l], log_dt[l], ssm_b_re[l], ssm_b_im[l],
                               ssm_c_re[l], ssm_c_im[l])
    later = [w_ff1[l], w_ff2[l], w_out[l], w_glu_val[l], w_glu_gate[l], w_pool_out[l]]
    z, w1_bf, w2_bf, wo_bf, wv_bf, wg_bf, wp_bf = _ssm(
        proj, wb, wct, a_all, ssm_d[l].reshape(SSM_GB, 1, SSM_CH), later, seq)

    yp = _pool(proj, w_pool[l].astype(BF16), pool_scale[l][None, :], seq)

    merged = _merge(z, yp, proj, wv_bf, wg_bf, wp_bf)

    x1, h2 = _out_ln(merged, wo_bf, x2, mod3, ln1_g[l][None, :], ln1_b[l][None, :], seq)

    out = _mlp(h2, w1_bf, w2_bf, x1, mod3, ln2_g[l][None, :], ln2_b[l][None, :], seq)
    return out.reshape(bsz, seq, d)
```

```python
import functools
import math

import jax
import jax.numpy as jnp
from jax import lax
from jax.experimental import pallas as pl
from jax.experimental.pallas import tpu as pltpu

D_MODEL = 2048
SSM_WIDTH = 1024
SSM_GROUP = 16
SSM_GROUPS = 64
SSM_STATE = 64
POOL_WIDTH = 1024
POOL_WINDOWS = (2, 4, 8, 16)
POOL_GROUP = 256
IN_WIDTH = SSM_WIDTH + POOL_WIDTH + 2 * D_MODEL
D_FF = 4 * D_MODEL
N_MOD = 6
MOD_SH1, MOD_SC1, MOD_G1, MOD_SH2, MOD_SC2, MOD_G2 = range(N_MOD)
LN_EPS = 1e-5
ALPHA = 2.0 ** 0.25

SUBLANES = 8
LANES = 128
MXU_TILE = 256
VMEM_LIMIT = 62 * 1024 * 1024

MOD_TN = 1024
IN_PROJ_TM, IN_PROJ_TN = 1024, 2048
POOL_TT = 1024
MERGE_TM, MERGE_TN = 1024, 2048
OUT_LN_TM = 512
MLP_TM, MLP_TF = 1024, 1024

SSM_CHUNK = 128
SSM_ROWS = SUBLANES * SSM_CHUNK
SSM_GB = 4
SSM_CH = SSM_WIDTH // SSM_GB
SSM_ST = SSM_CH // SSM_GROUP * SSM_STATE
SSM_COLS = 512

BF16 = jnp.bfloat16
F32 = jnp.float32


def _cparams(sem):
    return pltpu.CompilerParams(dimension_semantics=sem, vmem_limit_bytes=VMEM_LIMIT)


def _ln(x):
    mu = jnp.mean(x, axis=-1, keepdims=True)
    xc = x - mu
    var = jnp.mean(xc * xc, axis=-1, keepdims=True)
    return xc * lax.rsqrt(var + LN_EPS)


def _mod_kernel(c_ref, w_ref, b_ref, o_ref):
    c = c_ref[...]
    c_act = (c * jax.nn.sigmoid(c)).astype(BF16)
    o_ref[...] = jnp.dot(c_act, w_ref[...].astype(BF16),
                         preferred_element_type=F32) + b_ref[...]


def _mod(c_pad, w_ada, b_ada):
    tn = MOD_TN
    n = w_ada.shape[1]
    return pl.pallas_call(
        _mod_kernel,
        grid=(n // tn,),
        in_specs=[pl.BlockSpec((SUBLANES, D_MODEL), lambda j: (0, 0)),
                  pl.BlockSpec((D_MODEL, tn), lambda j: (0, j)),
                  pl.BlockSpec((1, tn), lambda j: (0, j))],
        out_specs=pl.BlockSpec((SUBLANES, tn), lambda j: (0, j)),
        out_shape=jax.ShapeDtypeStruct((SUBLANES, n), F32),
        compiler_params=_cparams(("arbitrary",)),
        name="mod",
    )(c_pad, w_ada, b_ada)


def _in_proj_kernel(x_ref, sh_ref, sc_ref, w_ref, proj_ref, h_ref):
    j = pl.program_id(1)
    half = x_ref.shape[0] // 2

    @pl.when(j == 0)
    def _():
        for s in range(2):
            rs = slice(s * half, (s + 1) * half)
            y = _ln(x_ref[rs, :])
            h_ref[rs, :] = (y * (1.0 + sc_ref[0]) + sh_ref[0]).astype(BF16)
            proj_ref[rs, :] = jnp.dot(h_ref[rs, :], w_ref[...],
                                      preferred_element_type=F32).astype(BF16)

    @pl.when(j != 0)
    def _():
        proj_ref[...] = jnp.dot(h_ref[...], w_ref[...],
                                preferred_element_type=F32).astype(BF16)


def _in_proj(x2, mod3, w_in, seq):
    tm, tn = IN_PROJ_TM, IN_PROJ_TN
    t = x2.shape[0]
    per_batch = seq // tm
    return pl.pallas_call(
        _in_proj_kernel,
        grid=(t // tm, IN_WIDTH // tn),
        in_specs=[pl.BlockSpec((tm, D_MODEL), lambda i, j: (i, 0)),
                  pl.BlockSpec((1, 1, D_MODEL), lambda i, j: (i // per_batch, 0, MOD_SH1)),
                  pl.BlockSpec((1, 1, D_MODEL), lambda i, j: (i // per_batch, 0, MOD_SC1)),
                  pl.BlockSpec((D_MODEL, tn), lambda i, j: (0, j))],
        out_specs=pl.BlockSpec((tm, tn), lambda i, j: (i, j)),
        out_shape=jax.ShapeDtypeStruct((t, IN_WIDTH), BF16),
        scratch_shapes=[pltpu.VMEM((tm, D_MODEL), BF16)],
        compiler_params=_cparams(("parallel", "arbitrary")),
        name="in_proj",
    )(x2, mod3, mod3, w_in)


def _zoh_pole(lr, li, dt, steps):
    mag = jnp.exp(lr * dt * steps)
    ang = li * dt * steps
    return mag * jnp.cos(ang), mag * jnp.sin(ang)


def _ssm_prep_kernel(lr_ref, li_ref, ldt_ref, br_ref, bi_ref, cr_ref, ci_ref,
                     lrs_ref, lis_ref, ldts_ref, wb_ref, wct_ref, a_ref):
    lr = lr_ref[...]
    li = li_ref[...]
    ab_re, ab_im = _zoh_pole(lr, li, jnp.exp(ldt_ref[...]), 1.0)
    num_re = ab_re - 1.0
    num_im = ab_im
    den = lr * lr + li * li
    f_re = (num_re * lr + num_im * li) / den
    f_im = (num_im * lr - num_re * li) / den
    br = br_ref[...]
    bi = bi_ref[...]
    bb_re = f_re * br - f_im * bi
    bb_im = f_re * bi + f_im * br

    state = lax.broadcasted_iota(jnp.int32, (SSM_STATE, SSM_ST), 0)
    lane = lax.broadcasted_iota(jnp.int32, (SSM_STATE, SSM_ST), 1)
    expand = ((lane & (SSM_STATE - 1)) == state).astype(BF16)
    row_g = lax.broadcasted_iota(jnp.int32, (SSM_CH, SSM_ST), 0) >> (SSM_GROUP.bit_length() - 1)
    col_g = lax.broadcasted_iota(jnp.int32, (SSM_CH, SSM_ST), 1) >> (SSM_STATE.bit_length() - 1)
    on_diag = row_g == col_g

    def place(w):
        t = jnp.dot(w.astype(BF16), expand, preferred_element_type=F32)
        return jnp.where(on_diag, t, 0.0).astype(BF16)

    wb_ref[0, :, 0:SSM_ST] = place(bb_re)
    wb_ref[0, :, SSM_ST:2 * SSM_ST] = place(bb_im)
    wct_ref[0, :, 0:SSM_ST] = place(cr_ref[...])
    wct_ref[0, :, SSM_ST:2 * SSM_ST] = place(-ci_ref[...])

    shape = (SUBLANES, SSM_ST)
    lrs = jnp.broadcast_to(lrs_ref[0], shape)
    lis = jnp.broadcast_to(lis_ref[0], shape)
    dts = jnp.exp(jnp.broadcast_to(ldts_ref[0], shape))
    a_ref[0, 0], a_ref[1, 0] = _zoh_pole(lrs, lis, dts, 1.0)
    a_ref[2, 0], a_ref[3, 0] = _zoh_pole(lrs, lis, dts, float(SSM_CHUNK))


def _ssm_prep(lam_re, lam_im, log_dt, b_re, b_im, c_re, c_im):
    rep = lambda a: jnp.repeat(a, SSM_GROUP, axis=0)
    ldt = jnp.broadcast_to(log_dt[:, None], (SSM_GROUPS, SSM_STATE))
    rows = [rep(lam_re), rep(lam_im), rep(ldt),
            b_re.transpose(0, 2, 1).reshape(SSM_WIDTH, SSM_STATE),
            b_im.transpose(0, 2, 1).reshape(SSM_WIDTH, SSM_STATE),
            c_re.reshape(SSM_WIDTH, SSM_STATE),
            c_im.reshape(SSM_WIDTH, SSM_STATE)]
    lanes = [a.reshape(SSM_GB, 1, SSM_ST) for a in (lam_re, lam_im, ldt)]
    row_spec = pl.BlockSpec((SSM_CH, SSM_STATE), lambda g: (g, 0))
    lane_spec = pl.BlockSpec((1, 1, SSM_ST), lambda g: (g, 0, 0))
    w_spec = pl.BlockSpec((1, SSM_CH, 2 * SSM_ST), lambda g: (g, 0, 0))
    w_shape = jax.ShapeDtypeStruct((SSM_GB, SSM_CH, 2 * SSM_ST), BF16)
    return pl.pallas_call(
        _ssm_prep_kernel,
        grid=(SSM_GB,),
        in_specs=[row_spec] * len(rows) + [lane_spec] * len(lanes),
        out_specs=[w_spec, w_spec,
                   pl.BlockSpec((4, 1, SUBLANES, SSM_ST), lambda g: (0, g, 0, 0))],
        out_shape=[w_shape, w_shape,
                   jax.ShapeDtypeStruct((4, SSM_GB, SUBLANES, SSM_ST), F32)],
        compiler_params=_cparams(("parallel",)),
        name="ssm_prep",
    )(*rows, *lanes)


def _sublane_transpose(blocks):
    sub = lax.broadcasted_iota(jnp.int32, blocks[0].shape, 1)
    a = list(blocks)
    for s in (1, 2, 4):
        keep = (sub & s) == 0
        nxt = list(a)
        for r in range(SUBLANES):
            if r & s == 0:
                lo, hi = a[r], a[r + s]
                nxt[r] = jnp.where(keep, lo, pltpu.roll(hi, s, 1))
                nxt[r + s] = jnp.where(keep, pltpu.roll(lo, SUBLANES - s, 1), hi)
        a = nxt
    return a


def _ssm_kernel(u_ref, wb_ref, wct_ref, a_ref, d_ref, *refs, n_cast):
    cast_in = refs[:n_cast]
    z_ref = refs[n_cast]
    cast_out = refs[n_cast + 1:2 * n_cast + 1]
    up_ref, upb_ref, bu_ref, xb_ref, cst_ref, st_ref = refs[2 * n_cast + 1:]
    k = pl.program_id(2)
    L = SSM_CHUNK
    lh = L // SUBLANES

    @pl.when(k == 0)
    def _():
        cst_ref[...] = jnp.zeros_like(cst_ref)

    u4 = u_ref[...].astype(F32).reshape(SUBLANES, lh, SUBLANES, SSM_CH)
    for l_lo, blk in enumerate(_sublane_transpose([u4[r] for r in range(SUBLANES)])):
        up_ref[:, l_lo] = blk
    upb_ref[...] = up_ref[...].reshape(SSM_ROWS, SSM_CH).astype(BF16)

    casts = list(zip(cast_in, cast_out))
    n_tiles = 2 * SSM_ST // MXU_TILE
    assert len(casts) <= n_tiles
    for n in range(n_tiles):
        cols = slice(n * MXU_TILE, (n + 1) * MXU_TILE)
        bu_ref[:, cols] = jnp.dot(upb_ref[...], wb_ref[0, :, cols],
                                  preferred_element_type=F32)
        if n < len(casts):
            src, dst = casts[n]
            dst[...] = src[...].astype(BF16)

    row_id = lax.broadcasted_iota(jnp.int32, (SUBLANES, SSM_COLS), 0)
    for cg in range(SSM_ST // SSM_COLS):
        re_sl = slice(cg * SSM_COLS, (cg + 1) * SSM_COLS)
        im_sl = slice(SSM_ST + cg * SSM_COLS, SSM_ST + (cg + 1) * SSM_COLS)
        ar = a_ref[0, 0, :, re_sl]
        ai = a_ref[1, 0, :, re_sl]

        def step(l, sr, si):
            row = pl.multiple_of(l * SUBLANES, SUBLANES)
            br = bu_ref[pl.ds(row, SUBLANES), re_sl]
            bi = bu_ref[pl.ds(row, SUBLANES), im_sl]
            return ar * sr - ai * si + br, ar * si + ai * sr + bi

        def pass1(l, carry):
            return step(l, *carry)
        er, ei = lax.fori_loop(0, L, pass1, (cst_ref[:, re_sl], cst_ref[:, im_sl]),
                               unroll=2)

        alr = a_ref[2, 0, :, re_sl]
        ali = a_ref[3, 0, :, re_sl]
        tr, ti = er, ei
        for r in range(1, SUBLANES):
            pr = pltpu.roll(tr, 1, 0)
            pi = pltpu.roll(ti, 1, 0)
            tr = jnp.where(row_id == r, er + alr * pr - ali * pi, tr)
            ti = jnp.where(row_id == r, ei + alr * pi + ali * pr, ti)
        pr = pltpu.roll(tr, 1, 0)
        pi = pltpu.roll(ti, 1, 0)
        first = row_id == 0
        st_ref[:, re_sl] = jnp.where(first, cst_ref[:, re_sl], pr)
        st_ref[:, im_sl] = jnp.where(first, cst_ref[:, im_sl], pi)
        cst_ref[:, re_sl] = jnp.where(first, pr, 0.0)
        cst_ref[:, im_sl] = jnp.where(first, pi, 0.0)

        def pass2(m, carry):
            sr, si = carry
            r1, i1 = step(2 * m, sr, si)
            r2, i2 = step(2 * m + 1, r1, i1)
            row = pl.multiple_of(m * 2 * SUBLANES, 2 * SUBLANES)
            xb_ref[pl.ds(row, 2 * SUBLANES), re_sl] = (
                jnp.concatenate([r1, r2], axis=0).astype(BF16))
            xb_ref[pl.ds(row, 2 * SUBLANES), im_sl] = (
                jnp.concatenate([i1, i2], axis=0).astype(BF16))
            return r2, i2
        lax.fori_loop(0, L // 2, pass2, (st_ref[:, re_sl], st_ref[:, im_sl]),
                      unroll=2)

    mrows = MXU_TILE
    mh = mrows // (SUBLANES * SUBLANES)
    for m in range(SSM_ROWS // mrows):
        rs = slice(m * mrows, (m + 1) * mrows)
        y = lax.dot_general(xb_ref[rs, :], wct_ref[0], (((1,), (1,)), ((), ())),
                            preferred_element_type=F32)
        y = y + d_ref[0] * up_ref[m * mh:(m + 1) * mh].reshape(mrows, SSM_CH)
        g4 = jax.nn.gelu(y, approximate=True).reshape(mh, SUBLANES, SUBLANES, SSM_CH)
        blocks = _sublane_transpose([g4[:, l_lo] for l_lo in range(SUBLANES)])
        for r, blk in enumerate(blocks):
            rows = slice(r * L + m * mh * SUBLANES, r * L + (m + 1) * mh * SUBLANES)
            z_ref[rows, :] = blk.reshape(mh * SUBLANES, SSM_CH).astype(BF16)


def _ssm(proj, wb, wct, a_all, d, weights, seq):
    t = proj.shape[0]
    nb = t // seq
    nk = seq // SSM_ROWS
    steps = SSM_GB * nb * nk
    tok = lambda g, b, k: (b * nk + k, g)
    par = lambda g, b, k: (g, 0, 0)
    step = lambda g, b, k: ((g * nb + b) * nk + k, 0)
    lh = SSM_CHUNK // SUBLANES
    cast_specs = [pl.BlockSpec((w.shape[0] // steps, w.shape[1]), step) for w in weights]
    return pl.pallas_call(
        functools.partial(_ssm_kernel, n_cast=len(weights)),
        grid=(SSM_GB, nb, nk),
        in_specs=[pl.BlockSpec((SSM_ROWS, SSM_CH), tok),
                  pl.BlockSpec((1, SSM_CH, 2 * SSM_ST), par),
                  pl.BlockSpec((1, SSM_CH, 2 * SSM_ST), par),
                  pl.BlockSpec((4, 1, SUBLANES, SSM_ST), lambda g, b, k: (0, g, 0, 0)),
                  pl.BlockSpec((1, 1, SSM_CH), par)] + cast_specs,
        out_specs=[pl.BlockSpec((SSM_ROWS, SSM_CH), tok)] + cast_specs,
        out_shape=[jax.ShapeDtypeStruct((t, SSM_WIDTH), BF16)]
        + [jax.ShapeDtypeStruct(w.shape, BF16) for w in weights],
        scratch_shapes=[pltpu.VMEM((lh, SUBLANES, SUBLANES, SSM_CH), F32),
                        pltpu.VMEM((SSM_ROWS, SSM_CH), BF16),
                        pltpu.VMEM((SSM_ROWS, 2 * SSM_ST), F32),
                        pltpu.VMEM((SSM_ROWS, 2 * SSM_ST), BF16),
                        pltpu.VMEM((SUBLANES, 2 * SSM_ST), F32),
                        pltpu.VMEM((SUBLANES, 2 * SSM_ST), F32)],
        compiler_params=_cparams(("arbitrary", "arbitrary", "arbitrary")),
        name="ssm",
    )(proj, wb, wct, a_all, d, *weights)


POOL_HALO = 32
POOL_LEAD = 16


def _pool_kernel(u_ref, w_ref, s_ref, o_ref, ext_ref, sum_ref, *, tt):
    k = pl.program_id(1)
    base = POOL_HALO - POOL_LEAD
    n = tt + POOL_LEAD

    @pl.when(k == 0)
    def _():
        ext_ref[0:POOL_HALO, :] = jnp.zeros((POOL_HALO, POOL_WIDTH), F32)

    @pl.when(k != 0)
    def _():
        ext_ref[0:POOL_HALO, :] = ext_ref[tt:tt + POOL_HALO, :]

    ext_ref[POOL_HALO:POOL_HALO + tt, :] = u_ref[...].astype(F32)
    sum_ref[0:base, :] = jnp.zeros((base, POOL_GROUP), F32)
    pos = (k * tt + lax.broadcasted_iota(jnp.int32, (tt, POOL_GROUP), 0)).astype(F32)
    for gi, w in enumerate(POOL_WINDOWS):
        cs = slice(gi * POOL_GROUP, (gi + 1) * POOL_GROUP)
        tot = ext_ref[base:base + n, cs] + ext_ref[base - 1:base - 1 + n, cs]
        m = 2
        while m < w:
            sum_ref[base:base + n, :] = tot
            tot = tot + sum_ref[base - m:base - m + n, :]
            m *= 2
        cur = ext_ref[POOL_HALO:POOL_HALO + tt, cs]
        count = jnp.minimum(pos + 1.0, float(w))
        pooled = tot[POOL_LEAD:, :] / count - cur
        y = jnp.dot(pooled.astype(BF16), w_ref[gi], preferred_element_type=F32)
        o_ref[:, cs] = (y * s_ref[:, cs]).astype(BF16)


def _pool(proj, w_pool, pool_scale, seq):
    tt = POOL_TT
    t = proj.shape[0]
    nk = seq // tt
    return pl.pallas_call(
        functools.partial(_pool_kernel, tt=tt),
        grid=(t // seq, nk),
        in_specs=[pl.BlockSpec((tt, POOL_WIDTH), lambda b, k: (b * nk + k, 1)),
                  pl.BlockSpec((len(POOL_WINDOWS), POOL_GROUP, POOL_GROUP),
                               lambda b, k: (0, 0, 0)),
                  pl.BlockSpec((1, POOL_WIDTH), lambda b, k: (0, 0))],
        out_specs=pl.BlockSpec((tt, POOL_WIDTH), lambda b, k: (b * nk + k, 0)),
        out_shape=jax.ShapeDtypeStruct((t, POOL_WIDTH), BF16),
        scratch_shapes=[pltpu.VMEM((tt + POOL_HALO, POOL_WIDTH), F32),
                        pltpu.VMEM((tt + POOL_HALO, POOL_GROUP), F32)],
        compiler_params=_cparams(("parallel", "arbitrary")),
        name="pool",
    )(proj, w_pool, pool_scale)


def _merge_kernel(z_ref, yp_ref, ga_ref, gb_ref, wv_ref, wg_ref, wp_ref, o_ref):
    half = z_ref.shape[0] // 2
    for s in range(2):
        rs = slice(s * half, (s + 1) * half)
        z = z_ref[rs, :]
        val = jnp.dot(z, wv_ref[...], preferred_element_type=F32)
        gate = jnp.dot(z, wg_ref[...], preferred_element_type=F32)
        y_a = val * jax.nn.sigmoid(gate)
        y_b = jnp.dot(yp_ref[rs, :], wp_ref[...], preferred_element_type=F32)
        merged = (jax.nn.sigmoid(ga_ref[rs, :].astype(F32)) * y_a
                  + jax.nn.sigmoid(gb_ref[rs, :].astype(F32)) * y_b)
        o_ref[rs, :] = merged.astype(BF16)


def _merge(z, yp, proj, w_val, w_gate, w_po):
    tm, tn = MERGE_TM, MERGE_TN
    t = z.shape[0]
    ga0 = (SSM_WIDTH + POOL_WIDTH) // tn
    gb0 = ga0 + D_MODEL // tn
    wspec = pl.BlockSpec((SSM_WIDTH, tn), lambda i, j: (0, j),
                         pipeline_mode=pl.Buffered(1) if tn == D_MODEL else None)
    return pl.pallas_call(
        _merge_kernel,
        grid=(t // tm, D_MODEL // tn),
        in_specs=[pl.BlockSpec((tm, SSM_WIDTH), lambda i, j: (i, 0)),
                  pl.BlockSpec((tm, POOL_WIDTH), lambda i, j: (i, 0)),
                  pl.BlockSpec((tm, tn), lambda i, j: (i, ga0 + j)),
                  pl.BlockSpec((tm, tn), lambda i, j: (i, gb0 + j)),
                  wspec, wspec, wspec],
        out_specs=pl.BlockSpec((tm, tn), lambda i, j: (i, j)),
        out_shape=jax.ShapeDtypeStruct((t, D_MODEL), BF16),
        compiler_params=_cparams(("parallel", "arbitrary")),
        name="merge",
    )(z, yp, proj, proj, w_val, w_gate, w_po)


def _out_ln_kernel(m_ref, w_ref, x_ref, g1_ref, lg_ref, lb_ref, sh_ref, sc_ref,
                   x1_ref, h2_ref):
    half = m_ref.shape[0] // 2
    for s in range(2):
        rs = slice(s * half, (s + 1) * half)
        y = jnp.dot(m_ref[rs, :], w_ref[...], preferred_element_type=F32)
        r = ALPHA * x_ref[rs, :] + g1_ref[0] * y
        x1 = _ln(r) * lg_ref[...] + lb_ref[...]
        x1_ref[rs, :] = x1
        h2_ref[rs, :] = (_ln(x1) * (1.0 + sc_ref[0]) + sh_ref[0]).astype(BF16)


def _out_ln(merged, w_out, x2, mod3, ln_g, ln_b, seq):
    tm = OUT_LN_TM
    t = x2.shape[0]
    per_batch = seq // tm
    row = pl.BlockSpec((tm, D_MODEL), lambda i: (i, 0))
    vec = pl.BlockSpec((1, D_MODEL), lambda i: (0, 0))
    bvec = lambda col: pl.BlockSpec((1, 1, D_MODEL), lambda i: (i // per_batch, 0, col))
    return pl.pallas_call(
        _out_ln_kernel,
        grid=(t // tm,),
        in_specs=[row,
                  pl.BlockSpec((D_MODEL, D_MODEL), lambda i: (0, 0),
                               pipeline_mode=pl.Buffered(1)),
                  row, bvec(MOD_G1), vec, vec, bvec(MOD_SH2), bvec(MOD_SC2)],
        out_specs=[row, row],
        out_shape=[jax.ShapeDtypeStruct((t, D_MODEL), F32),
                   jax.ShapeDtypeStruct((t, D_MODEL), BF16)],
        compiler_params=_cparams(("parallel",)),
        name="out_ln",
    )(merged, w_out, x2, mod3, ln_g, ln_b, mod3, mod3)


def _mlp_kernel(h_ref, w1_ref, w2_ref, x1_ref, g2_ref, lg_ref, lb_ref, o_ref):
    f = pl.program_id(1)
    last = pl.num_programs(1) - 1
    half = h_ref.shape[0] // 2

    def ff(rows):
        a = jnp.maximum(jnp.dot(h_ref[rows, :], w1_ref[...],
                                preferred_element_type=F32), 0.0)
        return jnp.dot((a * a).astype(BF16), w2_ref[...], preferred_element_type=F32)

    @pl.when(f == 0)
    def _():
        o_ref[...] = ff(slice(None))

    @pl.when(jnp.logical_and(f > 0, f < last))
    def _():
        o_ref[...] += ff(slice(None))

    @pl.when(f == last)
    def _():
        for s in range(2):
            rs = slice(s * half, (s + 1) * half)
            r = ALPHA * x1_ref[rs, :] + g2_ref[0] * (o_ref[rs, :] + ff(rs))
            o_ref[rs, :] = _ln(r) * lg_ref[...] + lb_ref[...]


def _mlp(h2, w1, w2, x1, mod3, ln_g, ln_b, seq):
    tm, tf = MLP_TM, MLP_TF
    t = h2.shape[0]
    per_batch = seq // tm
    once = pl.Buffered(1)
    vec = pl.BlockSpec((1, D_MODEL), lambda i, f: (0, 0))
    return pl.pallas_call(
        _mlp_kernel,
        grid=(t // tm, D_FF // tf),
        in_specs=[pl.BlockSpec((tm, D_MODEL), lambda i, f: (i, 0), pipeline_mode=once),
                  pl.BlockSpec((D_MODEL, tf), lambda i, f: (0, f)),
                  pl.BlockSpec((tf, D_MODEL), lambda i, f: (f, 0)),
                  pl.BlockSpec((tm, D_MODEL), lambda i, f: (i, 0), pipeline_mode=once),
                  pl.BlockSpec((1, 1, D_MODEL), lambda i, f: (i // per_batch, 0, MOD_G2)),
                  vec, vec],
        out_specs=pl.BlockSpec((tm, D_MODEL), lambda i, f: (i, 0)),
        out_shape=jax.ShapeDtypeStruct((t, D_MODEL), F32),
        compiler_params=_cparams(("parallel", "arbitrary")),
        name="mlp",
    )(h2, w1, w2, x1, mod3, ln_g, ln_b)


def kernel(x, c, w_ada, b_ada, w_in, lam_re, lam_im, log_dt, ssm_b_re, ssm_b_im, ssm_c_re, ssm_c_im, ssm_d, w_glu_val, w_glu_gate, w_pool, pool_scale, w_pool_out, w_out, ln1_g, ln1_b, w_ff1, w_ff2, ln2_g, ln2_b):
    bsz, seq, d = x.shape
    t = bsz * seq
    l = 0
    x2 = x.reshape(t, d)

    c_pad = jnp.zeros((SUBLANES, d), F32).at[:bsz].set(c)
    mod3 = _mod(c_pad, w_ada[l], b_ada[l][None, :]).reshape(SUBLANES, 1, N_MOD * d)

    proj = _in_proj(x2, mod3, w_in[l].astype(BF16), seq)

    wb, wct, a_all = _ssm_prep(lam_re[l], lam_im[l], log_dt[l], ssm_b_re[l], ssm_b_im[l],
                               ssm_c_re[l], ssm_c_im[l])
    later = [w_ff1[l], w_ff2[l], w_out[l], w_glu_val[l], w_glu_gate[l], w_pool_out[l]]
    z, w1_bf, w2_bf, wo_bf, wv_bf, wg_bf, wp_bf = _ssm(
        proj, wb, wct, a_all, ssm_d[l].reshape(SSM_GB, 1, SSM_CH), later, seq)

    yp = _pool(proj, w_pool[l].astype(BF16), pool_scale[l][None, :], seq)

    merged = _merge(z, yp, proj, wv_bf, wg_bf, wp_bf)

    x1, h2 = _out_ln(merged, wo_bf, x2, mod3, ln1_g[l][None, :], ln1_b[l][None, :], seq)

    out = _mlp(h2, w1_bf, w2_bf, x1, mod3, ln2_g[l][None, :], ln2_b[l][None, :], seq)
    return out.reshape(bsz, seq, d)
```

```python
import functools
import math

import jax
import jax.numpy as jnp
from jax import lax
from jax.experimental import pallas as pl
from jax.experimental.pallas import tpu as pltpu

D_MODEL = 2048
SSM_WIDTH = 1024
SSM_GROUP = 16
SSM_GROUPS = 64
SSM_STATE = 64
POOL_WIDTH = 1024
POOL_WINDOWS = (2, 4, 8, 16)
POOL_GROUP = 256
IN_WIDTH = SSM_WIDTH + POOL_WIDTH + 2 * D_MODEL
D_FF = 4 * D_MODEL
N_MOD = 6
MOD_SH1, MOD_SC1, MOD_G1, MOD_SH2, MOD_SC2, MOD_G2 = range(N_MOD)
LN_EPS = 1e-5
ALPHA = 2.0 ** 0.25

SUBLANES = 8
LANES = 128
MXU_TILE = 256
VMEM_LIMIT = 62 * 1024 * 1024

MOD_TN = 1024
IN_PROJ_TM, IN_PROJ_TN = 1024, 2048
POOL_TT = 2048
MERGE_TM, MERGE_TN = 1024, 1024
OUT_LN_TM = 512
MLP_TM, MLP_TF = 1024, 1024

SSM_CHUNK = 128
SSM_ROWS = SUBLANES * SSM_CHUNK
SSM_GB = 4
SSM_CH = SSM_WIDTH // SSM_GB
SSM_ST = SSM_CH // SSM_GROUP * SSM_STATE
SSM_COLS = 512

BF16 = jnp.bfloat16
F32 = jnp.float32


def _cparams(sem):
    return pltpu.CompilerParams(dimension_semantics=sem, vmem_limit_bytes=VMEM_LIMIT)


def _ln(x):
    mu = jnp.mean(x, axis=-1, keepdims=True)
    xc = x - mu
    var = jnp.mean(xc * xc, axis=-1, keepdims=True)
    return xc * lax.rsqrt(var + LN_EPS)


def _mod_kernel(c_ref, w_ref, b_ref, o_ref):
    c = c_ref[...]
    c_act = (c * jax.nn.sigmoid(c)).astype(BF16)
    o_ref[...] = jnp.dot(c_act, w_ref[...].astype(BF16),
                         preferred_element_type=F32) + b_ref[...]


def _mod(c_pad, w_ada, b_ada):
    tn = MOD_TN
    n = w_ada.shape[1]
    return pl.pallas_call(
        _mod_kernel,
        grid=(n // tn,),
        in_specs=[pl.BlockSpec((SUBLANES, D_MODEL), lambda j: (0, 0)),
                  pl.BlockSpec((D_MODEL, tn), lambda j: (0, j)),
                  pl.BlockSpec((1, tn), lambda j: (0, j))],
        out_specs=pl.BlockSpec((SUBLANES, tn), lambda j: (0, j)),
        out_shape=jax.ShapeDtypeStruct((SUBLANES, n), F32),
        compiler_params=_cparams(("arbitrary",)),
        name="mod",
    )(c_pad, w_ada, b_ada)


def _in_proj_kernel(x_ref, sh_ref, sc_ref, w_ref, proj_ref, h_ref):
    j = pl.program_id(1)
    half = x_ref.shape[0] // 2

    @pl.when(j == 0)
    def _():
        for s in range(2):
            rs = slice(s * half, (s + 1) * half)
            y = _ln(x_ref[rs, :])
            h_ref[rs, :] = (y * (1.0 + sc_ref[0]) + sh_ref[0]).astype(BF16)
            proj_ref[rs, :] = jnp.dot(h_ref[rs, :], w_ref[...],
                                      preferred_element_type=F32).astype(BF16)

    @pl.when(j != 0)
    def _():
        proj_ref[...] = jnp.dot(h_ref[...], w_ref[...],
                                preferred_element_type=F32).astype(BF16)


def _in_proj(x2, mod3, w_in, seq):
    tm, tn = IN_PROJ_TM, IN_PROJ_TN
    t = x2.shape[0]
    per_batch = seq // tm
    return pl.pallas_call(
        _in_proj_kernel,
        grid=(t // tm, IN_WIDTH // tn),
        in_specs=[pl.BlockSpec((tm, D_MODEL), lambda i, j: (i, 0)),
                  pl.BlockSpec((1, 1, D_MODEL), lambda i, j: (i // per_batch, 0, MOD_SH1)),
                  pl.BlockSpec((1, 1, D_MODEL), lambda i, j: (i // per_batch, 0, MOD_SC1)),
                  pl.BlockSpec((D_MODEL, tn), lambda i, j: (0, j))],
        out_specs=pl.BlockSpec((tm, tn), lambda i, j: (i, j)),
        out_shape=jax.ShapeDtypeStruct((t, IN_WIDTH), BF16),
        scratch_shapes=[pltpu.VMEM((tm, D_MODEL), BF16)],
        compiler_params=_cparams(("parallel", "arbitrary")),
        name="in_proj",
    )(x2, mod3, mod3, w_in)


def _zoh_pole(lr, li, dt, steps):
    mag = jnp.exp(lr * dt * steps)
    ang = li * dt * steps
    return mag * jnp.cos(ang), mag * jnp.sin(ang)


def _ssm_prep_kernel(lr_ref, li_ref, ldt_ref, br_ref, bi_ref, cr_ref, ci_ref,
                     lrs_ref, lis_ref, ldts_ref, wb_ref, wct_ref, a_ref):
    lr = lr_ref[...]
    li = li_ref[...]
    ab_re, ab_im = _zoh_pole(lr, li, jnp.exp(ldt_ref[...]), 1.0)
    num_re = ab_re - 1.0
    num_im = ab_im
    den = lr * lr + li * li
    f_re = (num_re * lr + num_im * li) / den
    f_im = (num_im * lr - num_re * li) / den
    br = br_ref[...]
    bi = bi_ref[...]
    bb_re = f_re * br - f_im * bi
    bb_im = f_re * bi + f_im * br

    state = lax.broadcasted_iota(jnp.int32, (SSM_STATE, SSM_ST), 0)
    lane = lax.broadcasted_iota(jnp.int32, (SSM_STATE, SSM_ST), 1)
    expand = ((lane & (SSM_STATE - 1)) == state).astype(BF16)
    row_g = lax.broadcasted_iota(jnp.int32, (SSM_CH, SSM_ST), 0) >> (SSM_GROUP.bit_length() - 1)
    col_g = lax.broadcasted_iota(jnp.int32, (SSM_CH, SSM_ST), 1) >> (SSM_STATE.bit_length() - 1)
    on_diag = row_g == col_g

    def place(w):
        t = jnp.dot(w.astype(BF16), expand, preferred_element_type=F32)
        return jnp.where(on_diag, t, 0.0).astype(BF16)

    wb_ref[0, :, 0:SSM_ST] = place(bb_re)
    wb_ref[0, :, SSM_ST:2 * SSM_ST] = place(bb_im)
    wct_ref[0, :, 0:SSM_ST] = place(cr_ref[...])
    wct_ref[0, :, SSM_ST:2 * SSM_ST] = place(-ci_ref[...])

    shape = (SUBLANES, SSM_ST)
    lrs = jnp.broadcast_to(lrs_ref[0], shape)
    lis = jnp.broadcast_to(lis_ref[0], shape)
    dts = jnp.exp(jnp.broadcast_to(ldts_ref[0], shape))
    a_ref[0, 0], a_ref[1, 0] = _zoh_pole(lrs, lis, dts, 1.0)
    a_ref[2, 0], a_ref[3, 0] = _zoh_pole(lrs, lis, dts, float(SSM_CHUNK))


def _ssm_prep(lam_re, lam_im, log_dt, b_re, b_im, c_re, c_im):
    rep = lambda a: jnp.repeat(a, SSM_GROUP, axis=0)
    ldt = jnp.broadcast_to(log_dt[:, None], (SSM_GROUPS, SSM_STATE))
    rows = [rep(lam_re), rep(lam_im), rep(ldt),
            b_re.transpose(0, 2, 1).reshape(SSM_WIDTH, SSM_STATE),
            b_im.transpose(0, 2, 1).reshape(SSM_WIDTH, SSM_STATE),
            c_re.reshape(SSM_WIDTH, SSM_STATE),
            c_im.reshape(SSM_WIDTH, SSM_STATE)]
    lanes = [a.reshape(SSM_GB, 1, SSM_ST) for a in (lam_re, lam_im, ldt)]
    row_spec = pl.BlockSpec((SSM_CH, SSM_STATE), lambda g: (g, 0))
    lane_spec = pl.BlockSpec((1, 1, SSM_ST), lambda g: (g, 0, 0))
    w_spec = pl.BlockSpec((1, SSM_CH, 2 * SSM_ST), lambda g: (g, 0, 0))
    w_shape = jax.ShapeDtypeStruct((SSM_GB, SSM_CH, 2 * SSM_ST), BF16)
    return pl.pallas_call(
        _ssm_prep_kernel,
        grid=(SSM_GB,),
        in_specs=[row_spec] * len(rows) + [lane_spec] * len(lanes),
        out_specs=[w_spec, w_spec,
                   pl.BlockSpec((4, 1, SUBLANES, SSM_ST), lambda g: (0, g, 0, 0))],
        out_shape=[w_shape, w_shape,
                   jax.ShapeDtypeStruct((4, SSM_GB, SUBLANES, SSM_ST), F32)],
        compiler_params=_cparams(("parallel",)),
        name="ssm_prep",
    )(*rows, *lanes)


def _sublane_transpose(blocks):
    sub = lax.broadcasted_iota(jnp.int32, blocks[0].shape, 1)
    a = list(blocks)
    for s in (1, 2, 4):
        keep = (sub & s) == 0
        nxt = list(a)
        for r in range(SUBLANES):
            if r & s == 0:
                lo, hi = a[r], a[r + s]
                nxt[r] = jnp.where(keep, lo, pltpu.roll(hi, s, 1))
                nxt[r + s] = jnp.where(keep, pltpu.roll(lo, SUBLANES - s, 1), hi)
        a = nxt
    return a


def _ssm_kernel(u_ref, wb_ref, wct_ref, a_ref, d_ref, *refs, n_cast):
    cast_in = refs[:n_cast]
    z_ref = refs[n_cast]
    cast_out = refs[n_cast + 1:2 * n_cast + 1]
    up_ref, upb_ref, bu_ref, xb_ref, cst_ref, st_ref = refs[2 * n_cast + 1:]
    k = pl.program_id(2)
    L = SSM_CHUNK
    lh = L // SUBLANES

    @pl.when(k == 0)
    def _():
        cst_ref[...] = jnp.zeros_like(cst_ref)

    u4 = u_ref[...].astype(F32).reshape(SUBLANES, lh, SUBLANES, SSM_CH)
    for l_lo, blk in enumerate(_sublane_transpose([u4[r] for r in range(SUBLANES)])):
        up_ref[:, l_lo] = blk
    upb_ref[...] = up_ref[...].reshape(SSM_ROWS, SSM_CH).astype(BF16)

    casts = list(zip(cast_in, cast_out))
    n_tiles = 2 * SSM_ST // MXU_TILE
    assert len(casts) <= n_tiles
    for n in range(n_tiles):
        cols = slice(n * MXU_TILE, (n + 1) * MXU_TILE)
        bu_ref[:, cols] = jnp.dot(upb_ref[...], wb_ref[0, :, cols],
                                  preferred_element_type=F32)
        if n < len(casts):
            src, dst = casts[n]
            dst[...] = src[...].astype(BF16)

    row_id = lax.broadcasted_iota(jnp.int32, (SUBLANES, SSM_COLS), 0)
    for cg in range(SSM_ST // SSM_COLS):
        re_sl = slice(cg * SSM_COLS, (cg + 1) * SSM_COLS)
        im_sl = slice(SSM_ST + cg * SSM_COLS, SSM_ST + (cg + 1) * SSM_COLS)
        ar = a_ref[0, 0, :, re_sl]
        ai = a_ref[1, 0, :, re_sl]

        def step(l, sr, si):
            row = pl.multiple_of(l * SUBLANES, SUBLANES)
            br = bu_ref[pl.ds(row, SUBLANES), re_sl]
            bi = bu_ref[pl.ds(row, SUBLANES), im_sl]
            return ar * sr - ai * si + br, ar * si + ai * sr + bi

        def pass1(l, carry):
            return step(l, *carry)
        er, ei = lax.fori_loop(0, L, pass1, (cst_ref[:, re_sl], cst_ref[:, im_sl]),
                               unroll=2)

        alr = a_ref[2, 0, :, re_sl]
        ali = a_ref[3, 0, :, re_sl]
        tr, ti = er, ei
        for r in range(1, SUBLANES):
            pr = pltpu.roll(tr, 1, 0)
            pi = pltpu.roll(ti, 1, 0)
            tr = jnp.where(row_id == r, er + alr * pr - ali * pi, tr)
            ti = jnp.where(row_id == r, ei + alr * pi + ali * pr, ti)
        pr = pltpu.roll(tr, 1, 0)
        pi = pltpu.roll(ti, 1, 0)
        first = row_id == 0
        st_ref[:, re_sl] = jnp.where(first, cst_ref[:, re_sl], pr)
        st_ref[:, im_sl] = jnp.where(first, cst_ref[:, im_sl], pi)
        cst_ref[:, re_sl] = jnp.where(first, pr, 0.0)
        cst_ref[:, im_sl] = jnp.where(first, pi, 0.0)

        def pass2(m, carry):
            sr, si = carry
            r1, i1 = step(2 * m, sr, si)
            r2, i2 = step(2 * m + 1, r1, i1)
            row = pl.multiple_of(m * 2 * SUBLANES, 2 * SUBLANES)
            xb_ref[pl.ds(row, 2 * SUBLANES), re_sl] = (
                jnp.concatenate([r1, r2], axis=0).astype(BF16))
            xb_ref[pl.ds(row, 2 * SUBLANES), im_sl] = (
                jnp.concatenate([i1, i2], axis=0).astype(BF16))
            return r2, i2
        lax.fori_loop(0, L // 2, pass2, (st_ref[:, re_sl], st_ref[:, im_sl]),
                      unroll=2)

    mrows = MXU_TILE
    mh = mrows // (SUBLANES * SUBLANES)
    for m in range(SSM_ROWS // mrows):
        rs = slice(m * mrows, (m + 1) * mrows)
        y = lax.dot_general(xb_ref[rs, :], wct_ref[0], (((1,), (1,)), ((), ())),
                            preferred_element_type=F32)
        y = y + d_ref[0] * up_ref[m * mh:(m + 1) * mh].reshape(mrows, SSM_CH)
        g4 = jax.nn.gelu(y, approximate=True).reshape(mh, SUBLANES, SUBLANES, SSM_CH)
        blocks = _sublane_transpose([g4[:, l_lo] for l_lo in range(SUBLANES)])
        for r, blk in enumerate(blocks):
            rows = slice(r * L + m * mh * SUBLANES, r * L + (m + 1) * mh * SUBLANES)
            z_ref[rows, :] = blk.reshape(mh * SUBLANES, SSM_CH).astype(BF16)


def _ssm(proj, wb, wct, a_all, d, weights, seq):
    t = proj.shape[0]
    nb = t // seq
    nk = seq // SSM_ROWS
    steps = SSM_GB * nb * nk
    tok = lambda g, b, k: (b * nk + k, g)
    par = lambda g, b, k: (g, 0, 0)
    step = lambda g, b, k: ((g * nb + b) * nk + k, 0)
    lh = SSM_CHUNK // SUBLANES
    cast_specs = [pl.BlockSpec((w.shape[0] // steps, w.shape[1]), step) for w in weights]
    return pl.pallas_call(
        functools.partial(_ssm_kernel, n_cast=len(weights)),
        grid=(SSM_GB, nb, nk),
        in_specs=[pl.BlockSpec((SSM_ROWS, SSM_CH), tok),
                  pl.BlockSpec((1, SSM_CH, 2 * SSM_ST), par),
                  pl.BlockSpec((1, SSM_CH, 2 * SSM_ST), par),
                  pl.BlockSpec((4, 1, SUBLANES, SSM_ST), lambda g, b, k: (0, g, 0, 0)),
                  pl.BlockSpec((1, 1, SSM_CH), par)] + cast_specs,
        out_specs=[pl.BlockSpec((SSM_ROWS, SSM_CH), tok)] + cast_specs,
        out_shape=[jax.ShapeDtypeStruct((t, SSM_WIDTH), BF16)]
        + [jax.ShapeDtypeStruct(w.shape, BF16) for w in weights],
        scratch_shapes=[pltpu.VMEM((lh, SUBLANES, SUBLANES, SSM_CH), F32),
                        pltpu.VMEM((SSM_ROWS, SSM_CH), BF16),
                        pltpu.VMEM((SSM_ROWS, 2 * SSM_ST), F32),
                        pltpu.VMEM((SSM_ROWS, 2 * SSM_ST), BF16),
                        pltpu.VMEM((SUBLANES, 2 * SSM_ST), F32),
                        pltpu.VMEM((SUBLANES, 2 * SSM_ST), F32)],
        compiler_params=_cparams(("arbitrary", "arbitrary", "arbitrary")),
        name="ssm",
    )(proj, wb, wct, a_all, d, *weights)


POOL_HALO = 32
POOL_LEAD = 16


def _pool_kernel(u_ref, w_ref, s_ref, o_ref, ext_ref, sum_ref, *, tt):
    k = pl.program_id(1)
    base = POOL_HALO - POOL_LEAD
    n = tt + POOL_LEAD

    @pl.when(k == 0)
    def _():
        ext_ref[0:POOL_HALO, :] = jnp.zeros((POOL_HALO, POOL_WIDTH), F32)

    @pl.when(k != 0)
    def _():
        ext_ref[0:POOL_HALO, :] = ext_ref[tt:tt + POOL_HALO, :]

    ext_ref[POOL_HALO:POOL_HALO + tt, :] = u_ref[...].astype(F32)
    sum_ref[0:base, :] = jnp.zeros((base, POOL_GROUP), F32)
    pos = (k * tt + lax.broadcasted_iota(jnp.int32, (tt, POOL_GROUP), 0)).astype(F32)
    for gi, w in enumerate(POOL_WINDOWS):
        cs = slice(gi * POOL_GROUP, (gi + 1) * POOL_GROUP)
        tot = ext_ref[base:base + n, cs] + ext_ref[base - 1:base - 1 + n, cs]
        m = 2
        while m < w:
            sum_ref[base:base + n, :] = tot
            tot = tot + sum_ref[base - m:base - m + n, :]
            m *= 2
        cur = ext_ref[POOL_HALO:POOL_HALO + tt, cs]
        count = jnp.minimum(pos + 1.0, float(w))
        pooled = tot[POOL_LEAD:, :] / count - cur
        y = jnp.dot(pooled.astype(BF16), w_ref[gi], preferred_element_type=F32)
        o_ref[:, cs] = (y * s_ref[:, cs]).astype(BF16)


def _pool(proj, w_pool, pool_scale, seq):
    tt = POOL_TT
    t = proj.shape[0]
    nk = seq // tt
    return pl.pallas_call(
        functools.partial(_pool_kernel, tt=tt),
        grid=(t // seq, nk),
        in_specs=[pl.BlockSpec((tt, POOL_WIDTH), lambda b, k: (b * nk + k, 1)),
                  pl.BlockSpec((len(POOL_WINDOWS), POOL_GROUP, POOL_GROUP),
                               lambda b, k: (0, 0, 0)),
                  pl.BlockSpec((1, POOL_WIDTH), lambda b, k: (0, 0))],
        out_specs=pl.BlockSpec((tt, POOL_WIDTH), lambda b, k: (b * nk + k, 0)),
        out_shape=jax.ShapeDtypeStruct((t, POOL_WIDTH), BF16),
        scratch_shapes=[pltpu.VMEM((tt + POOL_HALO, POOL_WIDTH), F32),
                        pltpu.VMEM((tt + POOL_HALO, POOL_GROUP), F32)],
        compiler_params=_cparams(("parallel", "arbitrary")),
        name="pool",
    )(proj, w_pool, pool_scale)


def _merge_kernel(z_ref, yp_ref, ga_ref, gb_ref, wv_ref, wg_ref, wp_ref, o_ref):
    half = z_ref.shape[0] // 2
    for s in range(2):
        rs = slice(s * half, (s + 1) * half)
        z = z_ref[rs, :]
        val = jnp.dot(z, wv_ref[...], preferred_element_type=F32)
        gate = jnp.dot(z, wg_ref[...], preferred_element_type=F32)
        y_a = val * jax.nn.sigmoid(gate)
        y_b = jnp.dot(yp_ref[rs, :], wp_ref[...], preferred_element_type=F32)
        merged = (jax.nn.sigmoid(ga_ref[rs, :].astype(F32)) * y_a
                  + jax.nn.sigmoid(gb_ref[rs, :].astype(F32)) * y_b)
        o_ref[rs, :] = merged.astype(BF16)


def _merge(z, yp, proj, w_val, w_gate, w_po):
    tm, tn = MERGE_TM, MERGE_TN
    t = z.shape[0]
    ga0 = (SSM_WIDTH + POOL_WIDTH) // tn
    gb0 = ga0 + D_MODEL // tn
    wspec = pl.BlockSpec((SSM_WIDTH, tn), lambda i, j: (0, j))
    return pl.pallas_call(
        _merge_kernel,
        grid=(t // tm, D_MODEL // tn),
        in_specs=[pl.BlockSpec((tm, SSM_WIDTH), lambda i, j: (i, 0)),
                  pl.BlockSpec((tm, POOL_WIDTH), lambda i, j: (i, 0)),
                  pl.BlockSpec((tm, tn), lambda i, j: (i, ga0 + j)),
                  pl.BlockSpec((tm, tn), lambda i, j: (i, gb0 + j)),
                  wspec, wspec, wspec],
        out_specs=pl.BlockSpec((tm, tn), lambda i, j: (i, j)),
        out_shape=jax.ShapeDtypeStruct((t, D_MODEL), BF16),
        compiler_params=_cparams(("parallel", "arbitrary")),
        name="merge",
    )(z, yp, proj, proj, w_val, w_gate, w_po)


def _out_ln_kernel(m_ref, w_ref, x_ref, g1_ref, lg_ref, lb_ref, sh_ref, sc_ref,
                   x1_ref, h2_ref):
    half = m_ref.shape[0] // 2
    for s in range(2):
        rs = slice(s * half, (s + 1) * half)
        y = jnp.dot(m_ref[rs, :], w_ref[...], preferred_element_type=F32)
        r = ALPHA * x_ref[rs, :] + g1_ref[0] * y
        x1 = _ln(r) * lg_ref[...] + lb_ref[...]
        x1_ref[rs, :] = x1
        h2_ref[rs, :] = (_ln(x1) * (1.0 + sc_ref[0]) + sh_ref[0]).astype(BF16)


def _out_ln(merged, w_out, x2, mod3, ln_g, ln_b, seq):
    tm = OUT_LN_TM
    t = x2.shape[0]
    per_batch = seq // tm
    row = pl.BlockSpec((tm, D_MODEL), lambda i: (i, 0))
    vec = pl.BlockSpec((1, D_MODEL), lambda i: (0, 0))
    bvec = lambda col: pl.BlockSpec((1, 1, D_MODEL), lambda i: (i // per_batch, 0, col))
    return pl.pallas_call(
        _out_ln_kernel,
        grid=(t // tm,),
        in_specs=[row,
                  pl.BlockSpec((D_MODEL, D_MODEL), lambda i: (0, 0),
                               pipeline_mode=pl.Buffered(1)),
                  row, bvec(MOD_G1), vec, vec, bvec(MOD_SH2), bvec(MOD_SC2)],
        out_specs=[row, row],
        out_shape=[jax.ShapeDtypeStruct((t, D_MODEL), F32),
                   jax.ShapeDtypeStruct((t, D_MODEL), BF16)],
        compiler_params=_cparams(("parallel",)),
        name="out_ln",
    )(merged, w_out, x2, mod3, ln_g, ln_b, mod3, mod3)


def _mlp_kernel(h_ref, w1_ref, w2_ref, x1_ref, g2_ref, lg_ref, lb_ref, o_ref):
    f = pl.program_id(1)
    last = pl.num_programs(1) - 1
    half = h_ref.shape[0] // 2

    def ff(rows):
        a = jnp.maximum(jnp.dot(h_ref[rows, :], w1_ref[...],
                                preferred_element_type=F32), 0.0)
        return jnp.dot((a * a).astype(BF16), w2_ref[...], preferred_element_type=F32)

    @pl.when(f == 0)
    def _():
        o_ref[...] = ff(slice(None))

    @pl.when(jnp.logical_and(f > 0, f < last))
    def _():
        o_ref[...] += ff(slice(None))

    @pl.when(f == last)
    def _():
        for s in range(2):
            rs = slice(s * half, (s + 1) * half)
            r = ALPHA * x1_ref[rs, :] + g2_ref[0] * (o_ref[rs, :] + ff(rs))
            o_ref[rs, :] = _ln(r) * lg_ref[...] + lb_ref[...]


def _mlp(h2, w1, w2, x1, mod3, ln_g, ln_b, seq):
    tm, tf = MLP_TM, MLP_TF
    t = h2.shape[0]
    per_batch = seq // tm
    once = pl.Buffered(1)
    vec = pl.BlockSpec((1, D_MODEL), lambda i, f: (0, 0))
    return pl.pallas_call(
        _mlp_kernel,
        grid=(t // tm, D_FF // tf),
        in_specs=[pl.BlockSpec((tm, D_MODEL), lambda i, f: (i, 0), pipeline_mode=once),
                  pl.BlockSpec((D_MODEL, tf), lambda i, f: (0, f)),
                  pl.BlockSpec((tf, D_MODEL), lambda i, f: (f, 0)),
                  pl.BlockSpec((tm, D_MODEL), lambda i, f: (i, 0), pipeline_mode=once),
                  pl.BlockSpec((1, 1, D_MODEL), lambda i, f: (i // per_batch, 0, MOD_G2)),
                  vec, vec],
        out_specs=pl.BlockSpec((tm, D_MODEL), lambda i, f: (i, 0)),
        out_shape=jax.ShapeDtypeStruct((t, D_MODEL), F32),
        compiler_params=_cparams(("parallel", "arbitrary")),
        name="mlp",
    )(h2, w1, w2, x1, mod3, ln_g, ln_b)


def kernel(x, c, w_ada, b_ada, w_in, lam_re, lam_im, log_dt, ssm_b_re, ssm_b_im, ssm_c_re, ssm_c_im, ssm_d, w_glu_val, w_glu_gate, w_pool, pool_scale, w_pool_out, w_out, ln1_g, ln1_b, w_ff1, w_ff2, ln2_g, ln2_b):
    bsz, seq, d = x.shape
    t = bsz * seq
    l = 0
    x2 = x.reshape(t, d)

    c_pad = jnp.zeros((SUBLANES, d), F32).at[:bsz].set(c)
    mod3 = _mod(c_pad, w_ada[l], b_ada[l][None, :]).reshape(SUBLANES, 1, N_MOD * d)

    proj = _in_proj(x2, mod3, w_in[l].astype(BF16), seq)

    wb, wct, a_all = _ssm_prep(lam_re[l], lam_im[l], log_dt[l], ssm_b_re[l], ssm_b_im[l],
                               ssm_c_re[l], ssm_c_im[l])
    later = [w_ff1[l], w_ff2[l], w_out[l], w_glu_val[l], w_glu_gate[l], w_pool_out[l]]
    z, w1_bf, w2_bf, wo_bf, wv_bf, wg_bf, wp_bf = _ssm(
        proj, wb, wct, a_all, ssm_d[l].reshape(SSM_GB, 1, SSM_CH), later, seq)

    yp = _pool(proj, w_pool[l].astype(BF16), pool_scale[l][None, :], seq)

    merged = _merge(z, yp, proj, wv_bf, wg_bf, wp_bf)

    x1, h2 = _out_ln(merged, wo_bf, x2, mod3, ln1_g[l][None, :], ln1_b[l][None, :], seq)

    out = _mlp(h2, w1_bf, w2_bf, x1, mod3, ln2_g[l][None, :], ln2_b[l][None, :], seq)
    return out.reshape(bsz, seq, d)
```

```python
import functools
import math

import jax
import jax.numpy as jnp
from jax import lax
from jax.experimental import pallas as pl
from jax.experimental.pallas import tpu as pltpu

D_MODEL = 2048
SSM_WIDTH = 1024
SSM_GROUP = 16
SSM_GROUPS = 64
SSM_STATE = 64
POOL_WIDTH = 1024
POOL_WINDOWS = (2, 4, 8, 16)
POOL_GROUP = 256
IN_WIDTH = SSM_WIDTH + POOL_WIDTH + 2 * D_MODEL
D_FF = 4 * D_MODEL
N_MOD = 6
MOD_SH1, MOD_SC1, MOD_G1, MOD_SH2, MOD_SC2, MOD_G2 = range(N_MOD)
LN_EPS = 1e-5
ALPHA = 2.0 ** 0.25

SUBLANES = 8
LANES = 128
MXU_TILE = 256
VMEM_LIMIT = 62 * 1024 * 1024

MOD_TN = 1024
IN_PROJ_TM, IN_PROJ_TN = 1024, 2048
POOL_TT = 2048
MERGE_TM, MERGE_TN = 1024, 1024
OUT_LN_TM = 512
MLP_TM, MLP_TF = 1024, 1024

SSM_CHUNK = 128
SSM_ROWS = SUBLANES * SSM_CHUNK
SSM_GB = 4
SSM_CH = SSM_WIDTH // SSM_GB
SSM_ST = SSM_CH // SSM_GROUP * SSM_STATE
SSM_COLS = 512

BF16 = jnp.bfloat16
F32 = jnp.float32


def _cparams(sem):
    return pltpu.CompilerParams(dimension_semantics=sem, vmem_limit_bytes=VMEM_LIMIT)


def _ln(x):
    mu = jnp.mean(x, axis=-1, keepdims=True)
    xc = x - mu
    var = jnp.mean(xc * xc, axis=-1, keepdims=True)
    return xc * lax.rsqrt(var + LN_EPS)


def _mod_kernel(c_ref, w_ref, b_ref, o_ref):
    c = c_ref[...]
    c_act = (c * jax.nn.sigmoid(c)).astype(BF16)
    o_ref[...] = jnp.dot(c_act, w_ref[...].astype(BF16),
                         preferred_element_type=F32) + b_ref[...]


def _mod(c_pad, w_ada, b_ada):
    tn = MOD_TN
    n = w_ada.shape[1]
    return pl.pallas_call(
        _mod_kernel,
        grid=(n // tn,),
        in_specs=[pl.BlockSpec((SUBLANES, D_MODEL), lambda j: (0, 0)),
                  pl.BlockSpec((D_MODEL, tn), lambda j: (0, j)),
                  pl.BlockSpec((1, tn), lambda j: (0, j))],
        out_specs=pl.BlockSpec((SUBLANES, tn), lambda j: (0, j)),
        out_shape=jax.ShapeDtypeStruct((SUBLANES, n), F32),
        compiler_params=_cparams(("arbitrary",)),
        name="mod",
    )(c_pad, w_ada, b_ada)


def _in_proj_kernel(x_ref, sh_ref, sc_ref, w_ref, proj_ref, h_ref):
    j = pl.program_id(1)
    half = x_ref.shape[0] // 2

    @pl.when(j == 0)
    def _():
        for s in range(2):
            rs = slice(s * half, (s + 1) * half)
            y = _ln(x_ref[rs, :])
            h_ref[rs, :] = (y * (1.0 + sc_ref[0]) + sh_ref[0]).astype(BF16)
            proj_ref[rs, :] = jnp.dot(h_ref[rs, :], w_ref[...],
                                      preferred_element_type=F32).astype(BF16)

    @pl.when(j != 0)
    def _():
        proj_ref[...] = jnp.dot(h_ref[...], w_ref[...],
                                preferred_element_type=F32).astype(BF16)


def _in_proj(x2, mod3, w_in, seq):
    tm, tn = IN_PROJ_TM, IN_PROJ_TN
    t = x2.shape[0]
    per_batch = seq // tm
    return pl.pallas_call(
        _in_proj_kernel,
        grid=(t // tm, IN_WIDTH // tn),
        in_specs=[pl.BlockSpec((tm, D_MODEL), lambda i, j: (i, 0)),
                  pl.BlockSpec((1, 1, D_MODEL), lambda i, j: (i // per_batch, 0, MOD_SH1)),
                  pl.BlockSpec((1, 1, D_MODEL), lambda i, j: (i // per_batch, 0, MOD_SC1)),
                  pl.BlockSpec((D_MODEL, tn), lambda i, j: (0, j))],
        out_specs=pl.BlockSpec((tm, tn), lambda i, j: (i, j)),
        out_shape=jax.ShapeDtypeStruct((t, IN_WIDTH), BF16),
        scratch_shapes=[pltpu.VMEM((tm, D_MODEL), BF16)],
        compiler_params=_cparams(("parallel", "arbitrary")),
        name="in_proj",
    )(x2, mod3, mod3, w_in)


def _zoh_pole(lr, li, dt, steps):
    mag = jnp.exp(lr * dt * steps)
    ang = li * dt * steps
    return mag * jnp.cos(ang), mag * jnp.sin(ang)


def _ssm_prep_kernel(lr_ref, li_ref, ldt_ref, br_ref, bi_ref, cr_ref, ci_ref,
                     lrs_ref, lis_ref, ldts_ref, wb_ref, wct_ref, a_ref):
    lr = lr_ref[...]
    li = li_ref[...]
    ab_re, ab_im = _zoh_pole(lr, li, jnp.exp(ldt_ref[...]), 1.0)
    num_re = ab_re - 1.0
    num_im = ab_im
    den = lr * lr + li * li
    f_re = (num_re * lr + num_im * li) / den
    f_im = (num_im * lr - num_re * li) / den
    br = br_ref[...]
    bi = bi_ref[...]
    bb_re = f_re * br - f_im * bi
    bb_im = f_re * bi + f_im * br

    state = lax.broadcasted_iota(jnp.int32, (SSM_STATE, SSM_ST), 0)
    lane = lax.broadcasted_iota(jnp.int32, (SSM_STATE, SSM_ST), 1)
    expand = ((lane & (SSM_STATE - 1)) == state).astype(BF16)
    row_g = lax.broadcasted_iota(jnp.int32, (SSM_CH, SSM_ST), 0) >> (SSM_GROUP.bit_length() - 1)
    col_g = lax.broadcasted_iota(jnp.int32, (SSM_CH, SSM_ST), 1) >> (SSM_STATE.bit_length() - 1)
    on_diag = row_g == col_g

    def place(w):
        t = jnp.dot(w.astype(BF16), expand, preferred_element_type=F32)
        return jnp.where(on_diag, t, 0.0).astype(BF16)

    wb_ref[0, :, 0:SSM_ST] = place(bb_re)
    wb_ref[0, :, SSM_ST:2 * SSM_ST] = place(bb_im)
    wct_ref[0, :, 0:SSM_ST] = place(cr_ref[...])
    wct_ref[0, :, SSM_ST:2 * SSM_ST] = place(-ci_ref[...])

    shape = (SUBLANES, SSM_ST)
    lrs = jnp.broadcast_to(lrs_ref[0], shape)
    lis = jnp.broadcast_to(lis_ref[0], shape)
    dts = jnp.exp(jnp.broadcast_to(ldts_ref[0], shape))
    a_ref[0, 0], a_ref[1, 0] = _zoh_pole(lrs, lis, dts, 1.0)
    a_ref[2, 0], a_ref[3, 0] = _zoh_pole(lrs, lis, dts, float(SSM_CHUNK))


def _ssm_prep(lam_re, lam_im, log_dt, b_re, b_im, c_re, c_im):
    rep = lambda a: jnp.repeat(a, SSM_GROUP, axis=0)
    ldt = jnp.broadcast_to(log_dt[:, None], (SSM_GROUPS, SSM_STATE))
    rows = [rep(lam_re), rep(lam_im), rep(ldt),
            b_re.transpose(0, 2, 1).reshape(SSM_WIDTH, SSM_STATE),
            b_im.transpose(0, 2, 1).reshape(SSM_WIDTH, SSM_STATE),
            c_re.reshape(SSM_WIDTH, SSM_STATE),
            c_im.reshape(SSM_WIDTH, SSM_STATE)]
    lanes = [a.reshape(SSM_GB, 1, SSM_ST) for a in (lam_re, lam_im, ldt)]
    row_spec = pl.BlockSpec((SSM_CH, SSM_STATE), lambda g: (g, 0))
    lane_spec = pl.BlockSpec((1, 1, SSM_ST), lambda g: (g, 0, 0))
    w_spec = pl.BlockSpec((1, SSM_CH, 2 * SSM_ST), lambda g: (g, 0, 0))
    w_shape = jax.ShapeDtypeStruct((SSM_GB, SSM_CH, 2 * SSM_ST), BF16)
    return pl.pallas_call(
        _ssm_prep_kernel,
        grid=(SSM_GB,),
        in_specs=[row_spec] * len(rows) + [lane_spec] * len(lanes),
        out_specs=[w_spec, w_spec,
                   pl.BlockSpec((4, 1, SUBLANES, SSM_ST), lambda g: (0, g, 0, 0))],
        out_shape=[w_shape, w_shape,
                   jax.ShapeDtypeStruct((4, SSM_GB, SUBLANES, SSM_ST), F32)],
        compiler_params=_cparams(("parallel",)),
        name="ssm_prep",
    )(*rows, *lanes)


def _sublane_transpose(blocks):
    sub = lax.broadcasted_iota(jnp.int32, blocks[0].shape, 1)
    a = list(blocks)
    for s in (1, 2, 4):
        keep = (sub & s) == 0
        nxt = list(a)
        for r in range(SUBLANES):
            if r & s == 0:
                lo, hi = a[r], a[r + s]
                nxt[r] = jnp.where(keep, lo, pltpu.roll(hi, s, 1))
                nxt[r + s] = jnp.where(keep, pltpu.roll(lo, SUBLANES - s, 1), hi)
        a = nxt
    return a


def _ssm_kernel(u_ref, wb_ref, wct_ref, a_ref, d_ref, *refs, n_cast):
    cast_in = refs[:n_cast]
    z_ref = refs[n_cast]
    cast_out = refs[n_cast + 1:2 * n_cast + 1]
    up_ref, upb_ref, bu_ref, xb_ref, cst_ref, st_ref = refs[2 * n_cast + 1:]
    k = pl.program_id(2)
    L = SSM_CHUNK
    lh = L // SUBLANES

    @pl.when(k == 0)
    def _():
        cst_ref[...] = jnp.zeros_like(cst_ref)

    u4 = u_ref[...].astype(F32).reshape(SUBLANES, lh, SUBLANES, SSM_CH)
    for l_lo, blk in enumerate(_sublane_transpose([u4[r] for r in range(SUBLANES)])):
        up_ref[:, l_lo] = blk
    upb_ref[...] = up_ref[...].reshape(SSM_ROWS, SSM_CH).astype(BF16)

    casts = list(zip(cast_in, cast_out))
    n_tiles = 2 * SSM_ST // MXU_TILE
    assert len(casts) <= n_tiles
    for n in range(n_tiles):
        cols = slice(n * MXU_TILE, (n + 1) * MXU_TILE)
        bu_ref[:, cols] = jnp.dot(upb_ref[...], wb_ref[0, :, cols],
                                  preferred_element_type=F32)
        if n < len(casts):
            src, dst = casts[n]
            dst[...] = src[...].astype(BF16)

    row_id = lax.broadcasted_iota(jnp.int32, (SUBLANES, SSM_COLS), 0)
    for cg in range(SSM_ST // SSM_COLS):
        re_sl = slice(cg * SSM_COLS, (cg + 1) * SSM_COLS)
        im_sl = slice(SSM_ST + cg * SSM_COLS, SSM_ST + (cg + 1) * SSM_COLS)
        ar = a_ref[0, 0, :, re_sl]
        ai = a_ref[1, 0, :, re_sl]

        def step(l, sr, si):
            row = pl.multiple_of(l * SUBLANES, SUBLANES)
            br = bu_ref[pl.ds(row, SUBLANES), re_sl]
            bi = bu_ref[pl.ds(row, SUBLANES), im_sl]
            return ar * sr - ai * si + br, ar * si + ai * sr + bi

        def pass1(l, carry):
            return step(l, *carry)
        er, ei = lax.fori_loop(0, L, pass1, (cst_ref[:, re_sl], cst_ref[:, im_sl]),
                               unroll=2)

        alr = a_ref[2, 0, :, re_sl]
        ali = a_ref[3, 0, :, re_sl]
        tr, ti = er, ei
        for r in range(1, SUBLANES):
            pr = pltpu.roll(tr, 1, 0)
            pi = pltpu.roll(ti, 1, 0)
            tr = jnp.where(row_id == r, er + alr * pr - ali * pi, tr)
            ti = jnp.where(row_id == r, ei + alr * pi + ali * pr, ti)
        pr = pltpu.roll(tr, 1, 0)
        pi = pltpu.roll(ti, 1, 0)
        first = row_id == 0
        st_ref[:, re_sl] = jnp.where(first, cst_ref[:, re_sl], pr)
        st_ref[:, im_sl] = jnp.where(first, cst_ref[:, im_sl], pi)
        cst_ref[:, re_sl] = jnp.where(first, pr, 0.0)
        cst_ref[:, im_sl] = jnp.where(first, pi, 0.0)

        def pass2(m, carry):
            sr, si = carry
            r1, i1 = step(2 * m, sr, si)
            r2, i2 = step(2 * m + 1, r1, i1)
            row = pl.multiple_of(m * 2 * SUBLANES, 2 * SUBLANES)
            xb_ref[pl.ds(row, 2 * SUBLANES), re_sl] = (
                jnp.concatenate([r1, r2], axis=0).astype(BF16))
            xb_ref[pl.ds(row, 2 * SUBLANES), im_sl] = (
                jnp.concatenate([i1, i2], axis=0).astype(BF16))
            return r2, i2
        lax.fori_loop(0, L // 2, pass2, (st_ref[:, re_sl], st_ref[:, im_sl]),
                      unroll=2)

    mrows = MXU_TILE
    mh = mrows // (SUBLANES * SUBLANES)
    for m in range(SSM_ROWS // mrows):
        rs = slice(m * mrows, (m + 1) * mrows)
        y = lax.dot_general(xb_ref[rs, :], wct_ref[0], (((1,), (1,)), ((), ())),
                            preferred_element_type=F32)
        y = y + d_ref[0] * up_ref[m * mh:(m + 1) * mh].reshape(mrows, SSM_CH)
        g4 = jax.nn.gelu(y, approximate=True).reshape(mh, SUBLANES, SUBLANES, SSM_CH)
        blocks = _sublane_transpose([g4[:, l_lo] for l_lo in range(SUBLANES)])
        for r, blk in enumerate(blocks):
            rows = slice(r * L + m * mh * SUBLANES, r * L + (m + 1) * mh * SUBLANES)
            z_ref[rows, :] = blk.reshape(mh * SUBLANES, SSM_CH).astype(BF16)


def _ssm(proj, wb, wct, a_all, d, weights, seq):
    t = proj.shape[0]
    nb = t // seq
    nk = seq // SSM_ROWS
    steps = SSM_GB * nb * nk
    tok = lambda g, b, k: (b * nk + k, g)
    par = lambda g, b, k: (g, 0, 0)
    step = lambda g, b, k: ((g * nb + b) * nk + k, 0)
    lh = SSM_CHUNK // SUBLANES
    cast_specs = [pl.BlockSpec((w.shape[0] // steps, w.shape[1]), step) for w in weights]
    return pl.pallas_call(
        functools.partial(_ssm_kernel, n_cast=len(weights)),
        grid=(SSM_GB, nb, nk),
        in_specs=[pl.BlockSpec((SSM_ROWS, SSM_CH), tok),
                  pl.BlockSpec((1, SSM_CH, 2 * SSM_ST), par),
                  pl.BlockSpec((1, SSM_CH, 2 * SSM_ST), par),
                  pl.BlockSpec((4, 1, SUBLANES, SSM_ST), lambda g, b, k: (0, g, 0, 0)),
                  pl.BlockSpec((1, 1, SSM_CH), par)] + cast_specs,
        out_specs=[pl.BlockSpec((SSM_ROWS, SSM_CH), tok)] + cast_specs,
        out_shape=[jax.ShapeDtypeStruct((t, SSM_WIDTH), BF16)]
        + [jax.ShapeDtypeStruct(w.shape, BF16) for w in weights],
        scratch_shapes=[pltpu.VMEM((lh, SUBLANES, SUBLANES, SSM_CH), F32),
                        pltpu.VMEM((SSM_ROWS, SSM_CH), BF16),
                        pltpu.VMEM((SSM_ROWS, 2 * SSM_ST), F32),
                        pltpu.VMEM((SSM_ROWS, 2 * SSM_ST), BF16),
                        pltpu.VMEM((SUBLANES, 2 * SSM_ST), F32),
                        pltpu.VMEM((SUBLANES, 2 * SSM_ST), F32)],
        compiler_params=_cparams(("arbitrary", "arbitrary", "arbitrary")),
        name="ssm",
    )(proj, wb, wct, a_all, d, *weights)


POOL_HALO = 32
POOL_LEAD = 16


def _pool_kernel(u_ref, w_ref, s_ref, o_ref, ext_ref, sum_ref, *, tt):
    k = pl.program_id(1)
    base = POOL_HALO - POOL_LEAD
    n = tt + POOL_LEAD

    @pl.when(k == 0)
    def _():
        ext_ref[0:POOL_HALO, :] = jnp.zeros((POOL_HALO, POOL_WIDTH), F32)

    @pl.when(k != 0)
    def _():
        ext_ref[0:POOL_HALO, :] = ext_ref[tt:tt + POOL_HALO, :]

    ext_ref[POOL_HALO:POOL_HALO + tt, :] = u_ref[...].astype(F32)
    sum_ref[0:base, :] = jnp.zeros((base, POOL_GROUP), F32)
    pos = (k * tt + lax.broadcasted_iota(jnp.int32, (tt, POOL_GROUP), 0)).astype(F32)
    for gi, w in enumerate(POOL_WINDOWS):
        cs = slice(gi * POOL_GROUP, (gi + 1) * POOL_GROUP)
        tot = ext_ref[base:base + n, cs] + ext_ref[base - 1:base - 1 + n, cs]
        m = 2
        while m < w:
            sum_ref[base:base + n, :] = tot
            tot = tot + sum_ref[base - m:base - m + n, :]
            m *= 2
        cur = ext_ref[POOL_HALO:POOL_HALO + tt, cs]
        count = jnp.minimum(pos + 1.0, float(w))
        pooled = tot[POOL_LEAD:, :] / count - cur
        y = jnp.dot(pooled.astype(BF16), w_ref[gi], preferred_element_type=F32)
        o_ref[:, cs] = (y * s_ref[:, cs]).astype(BF16)


def _pool(proj, w_pool, pool_scale, seq):
    tt = POOL_TT
    t = proj.shape[0]
    nk = seq // tt
    return pl.pallas_call(
        functools.partial(_pool_kernel, tt=tt),
        grid=(t // seq, nk),
        in_specs=[pl.BlockSpec((tt, POOL_WIDTH), lambda b, k: (b * nk + k, 1)),
                  pl.BlockSpec((len(POOL_WINDOWS), POOL_GROUP, POOL_GROUP),
                               lambda b, k: (0, 0, 0)),
                  pl.BlockSpec((1, POOL_WIDTH), lambda b, k: (0, 0))],
        out_specs=pl.BlockSpec((tt, POOL_WIDTH), lambda b, k: (b * nk + k, 0)),
        out_shape=jax.ShapeDtypeStruct((t, POOL_WIDTH), BF16),
        scratch_shapes=[pltpu.VMEM((tt + POOL_HALO, POOL_WIDTH), F32),
                        pltpu.VMEM((tt + POOL_HALO, POOL_GROUP), F32)],
        compiler_params=_cparams(("parallel", "arbitrary")),
        name="pool",
    )(proj, w_pool, pool_scale)


def _merge_kernel(z_ref, yp_ref, ga_ref, gb_ref, wv_ref, wg_ref, wp_ref, o_ref):
    half = z_ref.shape[0] // 2
    for s in range(2):
        rs = slice(s * half, (s + 1) * half)
        z = z_ref[rs, :]
        val = jnp.dot(z, wv_ref[...], preferred_element_type=F32)
        gate = jnp.dot(z, wg_ref[...], preferred_element_type=F32)
        y_a = val * jax.nn.sigmoid(gate)
        y_b = jnp.dot(yp_ref[rs, :], wp_ref[...], preferred_element_type=F32)
        merged = (jax.nn.sigmoid(ga_ref[rs, :].astype(F32)) * y_a
                  + jax.nn.sigmoid(gb_ref[rs, :].astype(F32)) * y_b)
        o_ref[rs, :] = merged.astype(BF16)


def _merge(z, yp, proj, w_val, w_gate, w_po):
    tm, tn = MERGE_TM, MERGE_TN
    t = z.shape[0]
    ga0 = (SSM_WIDTH + POOL_WIDTH) // tn
    gb0 = ga0 + D_MODEL // tn
    wspec = pl.BlockSpec((SSM_WIDTH, tn), lambda i, j: (0, j))
    return pl.pallas_call(
        _merge_kernel,
        grid=(t // tm, D_MODEL // tn),
        in_specs=[pl.BlockSpec((tm, SSM_WIDTH), lambda i, j: (i, 0)),
                  pl.BlockSpec((tm, POOL_WIDTH), lambda i, j: (i, 0)),
                  pl.BlockSpec((tm, tn), lambda i, j: (i, ga0 + j)),
                  pl.BlockSpec((tm, tn), lambda i, j: (i, gb0 + j)),
                  wspec, wspec, wspec],
        out_specs=pl.BlockSpec((tm, tn), lambda i, j: (i, j)),
        out_shape=jax.ShapeDtypeStruct((t, D_MODEL), BF16),
        compiler_params=_cparams(("parallel", "arbitrary")),
        name="merge",
    )(z, yp, proj, proj, w_val, w_gate, w_po)


def _out_ln_kernel(m_ref, w_ref, x_ref, g1_ref, lg_ref, lb_ref, sh_ref, sc_ref,
                   x1_ref, h2_ref):
    half = m_ref.shape[0] // 2
    for s in range(2):
        rs = slice(s * half, (s + 1) * half)
        y = jnp.dot(m_ref[rs, :], w_ref[...], preferred_element_type=F32)
        r = ALPHA * x_ref[rs, :] + g1_ref[0] * y
        x1 = _ln(r) * lg_ref[...] + lb_ref[...]
        x1_ref[rs, :] = x1
        h2_ref[rs, :] = (_ln(x1) * (1.0 + sc_ref[0]) + sh_ref[0]).astype(BF16)


def _out_ln(merged, w_out, x2, mod3, ln_g, ln_b, seq):
    tm = OUT_LN_TM
    t = x2.shape[0]
    per_batch = seq // tm
    row = pl.BlockSpec((tm, D_MODEL), lambda i: (i, 0))
    vec = pl.BlockSpec((1, D_MODEL), lambda i: (0, 0))
    bvec = lambda col: pl.BlockSpec((1, 1, D_MODEL), lambda i: (i // per_batch, 0, col))
    return pl.pallas_call(
        _out_ln_kernel,
        grid=(t // tm,),
        in_specs=[row,
                  pl.BlockSpec((D_MODEL, D_MODEL), lambda i: (0, 0),
                               pipeline_mode=pl.Buffered(1)),
                  row, bvec(MOD_G1), vec, vec, bvec(MOD_SH2), bvec(MOD_SC2)],
        out_specs=[row, row],
        out_shape=[jax.ShapeDtypeStruct((t, D_MODEL), F32),
                   jax.ShapeDtypeStruct((t, D_MODEL), BF16)],
        compiler_params=_cparams(("parallel",)),
        name="out_ln",
    )(merged, w_out, x2, mod3, ln_g, ln_b, mod3, mod3)


def _mlp_kernel(h_ref, w1_ref, w2_ref, x1_hbm, g2_ref, lg_ref, lb_ref, o_ref,
                x1_ref, x1_sem):
    f = pl.program_id(1)
    last = pl.num_programs(1) - 1
    tm = h_ref.shape[0]
    half = tm // 2

    def x1_copy():
        rows = pl.ds(pl.multiple_of(pl.program_id(0) * tm, tm), tm)
        return pltpu.make_async_copy(x1_hbm.at[rows, :], x1_ref, x1_sem)

    @pl.when(f == 0)
    def _():
        x1_copy().start()

    def ff(rows):
        a = jnp.maximum(jnp.dot(h_ref[rows, :], w1_ref[...],
                                preferred_element_type=F32), 0.0)
        return jnp.dot((a * a).astype(BF16), w2_ref[...], preferred_element_type=F32)

    @pl.when(f == 0)
    def _():
        o_ref[...] = ff(slice(None))

    @pl.when(jnp.logical_and(f > 0, f < last))
    def _():
        o_ref[...] += ff(slice(None))

    @pl.when(f == last)
    def _():
        x1_copy().wait()
        for s in range(2):
            rs = slice(s * half, (s + 1) * half)
            r = ALPHA * x1_ref[rs, :] + g2_ref[0] * (o_ref[rs, :] + ff(rs))
            o_ref[rs, :] = _ln(r) * lg_ref[...] + lb_ref[...]


def _mlp(h2, w1, w2, x1, mod3, ln_g, ln_b, seq):
    tm, tf = MLP_TM, MLP_TF
    t = h2.shape[0]
    per_batch = seq // tm
    once = pl.Buffered(1)
    vec = pl.BlockSpec((1, D_MODEL), lambda i, f: (0, 0))
    return pl.pallas_call(
        _mlp_kernel,
        grid=(t // tm, D_FF // tf),
        in_specs=[pl.BlockSpec((tm, D_MODEL), lambda i, f: (i, 0), pipeline_mode=once),
                  pl.BlockSpec((D_MODEL, tf), lambda i, f: (0, f)),
                  pl.BlockSpec((tf, D_MODEL), lambda i, f: (f, 0)),
                  pl.BlockSpec(memory_space=pl.ANY),
                  pl.BlockSpec((1, 1, D_MODEL), lambda i, f: (i // per_batch, 0, MOD_G2)),
                  vec, vec],
        out_specs=pl.BlockSpec((tm, D_MODEL), lambda i, f: (i, 0)),
        out_shape=jax.ShapeDtypeStruct((t, D_MODEL), F32),
        scratch_shapes=[pltpu.VMEM((tm, D_MODEL), F32), pltpu.SemaphoreType.DMA(())],
        compiler_params=_cparams(("arbitrary", "arbitrary")),
        name="mlp",
    )(h2, w1, w2, x1, mod3, ln_g, ln_b)


def kernel(x, c, w_ada, b_ada, w_in, lam_re, lam_im, log_dt, ssm_b_re, ssm_b_im, ssm_c_re, ssm_c_im, ssm_d, w_glu_val, w_glu_gate, w_pool, pool_scale, w_pool_out, w_out, ln1_g, ln1_b, w_ff1, w_ff2, ln2_g, ln2_b):
    bsz, seq, d = x.shape
    t = bsz * seq
    l = 0
    x2 = x.reshape(t, d)

    c_pad = jnp.zeros((SUBLANES, d), F32).at[:bsz].set(c)
    mod3 = _mod(c_pad, w_ada[l], b_ada[l][None, :]).reshape(SUBLANES, 1, N_MOD * d)

    proj = _in_proj(x2, mod3, w_in[l].astype(BF16), seq)

    wb, wct, a_all = _ssm_prep(lam_re[l], lam_im[l], log_dt[l], ssm_b_re[l], ssm_b_im[l],
                               ssm_c_re[l], ssm_c_im[l])
    later = [w_ff1[l], w_ff2[l], w_out[l], w_glu_val[l], w_glu_gate[l], w_pool_out[l]]
    z, w1_bf, w2_bf, wo_bf, wv_bf, wg_bf, wp_bf = _ssm(
        proj, wb, wct, a_all, ssm_d[l].reshape(SSM_GB, 1, SSM_CH), later, seq)

    yp = _pool(proj, w_pool[l].astype(BF16), pool_scale[l][None, :], seq)

    merged = _merge(z, yp, proj, wv_bf, wg_bf, wp_bf)

    x1, h2 = _out_ln(merged, wo_bf, x2, mod3, ln1_g[l][None, :], ln1_b[l][None, :], seq)

    out = _mlp(h2, w1_bf, w2_bf, x1, mod3, ln2_g[l][None, :], ln2_b[l][None, :], seq)
    return out.reshape(bsz, seq, d)
```
